```python
import jax, jax.numpy as jnp
from jax import lax
import numpy as np

D_MODEL = 1024
BATCH = 2
SEQ = 16384
DEPTH = 1
DEC_BATCH = 128
DEC_SEQ = 1
PAST_LEN = 8192
PAGE_SIZE = 128

MIX_WIDTH = D_MODEL
ATTN_WIDTH = MIX_WIDTH // 2
CONV_WIDTH = MIX_WIDTH - ATTN_WIDTH
HEAD_DIM = 64
N_HEADS = ATTN_WIDTH // HEAD_DIM
CONV_W = 3
D_FF = 4 * D_MODEL
Q_BLOCK = 128
EPS = 1e-6
LOGIT_BIAS_INIT = -8.0
PROJ_WIDTH = 3 * ATTN_WIDTH + 3 * CONV_WIDTH
SPLIT_POINTS = (ATTN_WIDTH, 2 * ATTN_WIDTH, 3 * ATTN_WIDTH,
                3 * ATTN_WIDTH + CONV_WIDTH, 3 * ATTN_WIDTH + 2 * CONV_WIDTH)

kernel_name = "hymba_stickbreak_shortconv_decode_step"


def rmsnorm(x, g):
    xf = x.astype(jnp.float32)
    y = xf * lax.rsqrt(jnp.mean(xf * xf, axis=-1, keepdims=True) + EPS)
    return (y * g.astype(jnp.float32)).astype(x.dtype)


def stick_breaking(q, k, v, bias, q_pos, k_pos):
    z = jnp.einsum('bqhd,bkhd->bhqk', q, k).astype(jnp.float32) * (HEAD_DIM ** -0.5)
    z = z + bias.astype(jnp.float32)[None, :, None, None]
    causal = k_pos[None, :] < q_pos[:, None]
    log_beta = jax.nn.log_sigmoid(z)
    log_1m = jnp.where(causal, jax.nn.log_sigmoid(-z), 0.0)
    suffix = lax.cumsum(log_1m, axis=3, reverse=True) - log_1m
    w = jnp.where(causal, jnp.exp(log_beta + suffix), 0.0)
    return jnp.einsum('bhqk,bkhd->bqhd', w.astype(v.dtype), v)


def prompt_attention(q, k, v, bias):
    b, t, h, d = q.shape
    nb = t // Q_BLOCK
    qb = q.reshape(b, nb, Q_BLOCK, h, d).transpose(1, 0, 2, 3, 4)
    k_pos = jnp.arange(t)

    def one_block(args):
        q_blk, i = args
        q_pos = i * Q_BLOCK + jnp.arange(Q_BLOCK)
        return stick_breaking(q_blk, k, v, bias, q_pos, k_pos)

    out = lax.map(one_block, (qb, jnp.arange(nb)))
    return out.transpose(1, 0, 2, 3, 4).reshape(b, t, h, d)


def sample_attention(q, k_new, v_new, bias, cache_k, cache_v, page_table):
    db, s = q.shape[0], q.shape[1]
    n_pages = page_table.shape[1]
    page = cache_k.shape[1]
    past = n_pages * page
    k_past = cache_k[page_table].reshape(db, past, N_HEADS, HEAD_DIM)
    v_past = cache_v[page_table].reshape(db, past, N_HEADS, HEAD_DIM)
    k_all = jnp.concatenate([k_past, k_new.astype(k_past.dtype)], axis=1)
    v_all = jnp.concatenate([v_past, v_new.astype(v_past.dtype)], axis=1)
    q_pos = past + jnp.arange(s)
    k_pos = jnp.arange(past + s)
    return stick_breaking(q, k_all, v_all, bias, q_pos, k_pos)


def causal_conv(u_ext, w):
    t = u_ext.shape[1] - (CONV_W - 1)
    y = w[0] * u_ext[:, 0:t]
    for i in range(1, CONV_W):
        y = y + w[i] * u_ext[:, i:i + t]
    return y


def layer(x, conv_prev, attn_fn, norm_pre_mix, norm_post_mix, norm_pre_mlp, norm_post_mlp,
          w_in, attn_logit_bias, conv_w, norm_attn_out, norm_conv_out, w_out, w_up, w_down):
    b, t, _ = x.shape
    h = rmsnorm(x, norm_pre_mix)
    proj = jnp.einsum('btd,de->bte', h, w_in)
    q, k, v, gb, gc, u = jnp.split(proj, SPLIT_POINTS, axis=-1)
    q = q.reshape(b, t, N_HEADS, HEAD_DIM)
    k = k.reshape(b, t, N_HEADS, HEAD_DIM)
    v = v.reshape(b, t, N_HEADS, HEAD_DIM)
    a = attn_fn(q, k, v, attn_logit_bias).reshape(b, t, ATTN_WIDTH)
    cu = gc * u
    cu_ext = jnp.concatenate([conv_prev.astype(cu.dtype), cu], axis=1)
    s = gb * causal_conv(cu_ext, conv_w)
    new_conv = cu_ext[:, -(CONV_W - 1):]
    merged = jnp.concatenate([rmsnorm(a, norm_attn_out), rmsnorm(s, norm_conv_out)], axis=-1)
    mix = jnp.einsum('bte,ed->btd', merged, w_out)
    x = x + rmsnorm(mix, norm_post_mix)
    hm = rmsnorm(x, norm_pre_mlp)
    up = jax.nn.relu(jnp.einsum('btd,df->btf', hm, w_up))
    m = jnp.einsum('btf,fd->btd', up * up, w_down)
    x = x + rmsnorm(m, norm_post_mlp)
    return x, k, v, new_conv


def setup_inputs(seed: int = 0) -> dict:
    key = jax.random.key(seed)
    ks = jax.random.split(key, 20)
    n_pages = PAST_LEN // PAGE_SIZE
    n_used = DEC_BATCH * n_pages
    n_phys = (n_used * 5) // 4
    f32 = jnp.float32

    def gain(k, n):
        return jnp.ones((DEPTH, n), f32) + 0.02 * jax.random.normal(k, (DEPTH, n), f32)

    x_prompt = jax.random.normal(ks[0], (BATCH, SEQ, D_MODEL), f32)
    x_sample = jax.random.normal(ks[1], (DEC_BATCH, DEC_SEQ, D_MODEL), f32)
    cache_k = jax.random.normal(ks[2], (DEPTH, n_phys, PAGE_SIZE, N_HEADS, HEAD_DIM), f32)
    cache_v = jax.random.normal(ks[3], (DEPTH, n_phys, PAGE_SIZE, N_HEADS, HEAD_DIM), f32)
    state_conv = jax.random.normal(ks[4], (DEPTH, DEC_BATCH, CONV_W - 1, CONV_WIDTH), f32)
    page_table = jax.random.permutation(ks[5], n_phys)[:n_used].reshape(DEC_BATCH, n_pages).astype(jnp.int32)
    return {
        "x_prompt": x_prompt,
        "x_sample": x_sample,
        "cache_k": cache_k,
        "cache_v": cache_v,
        "state_conv": state_conv,
        "page_table": page_table,
        "norm_pre_mix": gain(ks[6], D_MODEL),
        "norm_post_mix": gain(ks[7], D_MODEL),
        "norm_pre_mlp": gain(ks[8], D_MODEL),
        "norm_post_mlp": gain(ks[9], D_MODEL),
        "w_in": jax.random.normal(ks[10], (DEPTH, D_MODEL, PROJ_WIDTH), f32) * D_MODEL ** -0.5,
        "attn_logit_bias": LOGIT_BIAS_INIT + 0.5 * jax.random.normal(ks[17], (DEPTH, N_HEADS), f32),
        "conv_w": jax.random.normal(ks[11], (DEPTH, CONV_W, CONV_WIDTH), f32) * CONV_W ** -0.5,
        "norm_attn_out": gain(ks[12], ATTN_WIDTH),
        "norm_conv_out": gain(ks[13], CONV_WIDTH),
        "w_out": jax.random.normal(ks[14], (DEPTH, MIX_WIDTH, D_MODEL), f32) * MIX_WIDTH ** -0.5,
        "w_up": jax.random.normal(ks[15], (DEPTH, D_MODEL, D_FF), f32) * D_MODEL ** -0.5,
        "w_down": jax.random.normal(ks[16], (DEPTH, D_FF, D_MODEL), f32) * D_FF ** -0.5,
    }


def reference(x_prompt, x_sample, cache_k, cache_v, state_conv, page_table,
              norm_pre_mix, norm_post_mix, norm_pre_mlp, norm_post_mlp,
              w_in, attn_logit_bias, conv_w, norm_attn_out, norm_conv_out, w_out, w_up, w_down):
    xp = x_prompt
    xs = x_sample
    kp_list, vp_list, cp_list, ks_list, vs_list, cs_list = [], [], [], [], [], []
    for l in range(DEPTH):
        conv_zero = jnp.zeros((xp.shape[0], CONV_W - 1, CONV_WIDTH), xp.dtype)
        xp, kp, vp, cp = layer(
            xp, conv_zero, prompt_attention,
            norm_pre_mix[l], norm_post_mix[l], norm_pre_mlp[l], norm_post_mlp[l],
            w_in[l], attn_logit_bias[l], conv_w[l], norm_attn_out[l], norm_conv_out[l],
            w_out[l], w_up[l], w_down[l])
        ck_l = cache_k[l]
        cv_l = cache_v[l]

        def attn_sample(q, k, v, bias, ck_l=ck_l, cv_l=cv_l):
            return sample_attention(q, k, v, bias, ck_l, cv_l, page_table)

        xs, k_s, v_s, c_s = layer(
            xs, state_conv[l], attn_sample,
            norm_pre_mix[l], norm_post_mix[l], norm_pre_mlp[l], norm_post_mlp[l],
            w_in[l], attn_logit_bias[l], conv_w[l], norm_attn_out[l], norm_conv_out[l],
            w_out[l], w_up[l], w_down[l])
        kp_list.append(kp); vp_list.append(vp); cp_list.append(cp)
        ks_list.append(k_s); vs_list.append(v_s); cs_list.append(c_s)
    new_k_prompt = jnp.stack(kp_list)
    new_v_prompt = jnp.stack(vp_list)
    new_conv_prompt = jnp.stack(cp_list)
    new_k_sample = jnp.stack(ks_list)
    new_v_sample = jnp.stack(vs_list)
    new_conv_sample = jnp.stack(cs_list)
    return (xp, xs, new_k_prompt, new_v_prompt, new_conv_prompt, new_k_sample, new_v_sample, new_conv_sample)
```

```python
import functools

import jax
import jax.numpy as jnp
from jax import lax
from jax.experimental import pallas as pl
from jax.experimental.pallas import tpu as pltpu

HEAD_DIM = 64
CONV_TAPS = 3
EPS = 1e-6
SUBLANES = 8
LANES = 128
VMEM_LIMIT_BYTES = 56 * 1024 * 1024

F32 = jnp.float32
BF16 = jnp.bfloat16


def _rms(x, g):
    return x * lax.rsqrt(jnp.mean(x * x, axis=-1, keepdims=True) + EPS) * g


def _dot(a, b):
    return jnp.dot(a, b, preferred_element_type=F32)


def _dot_nt(a, b):
    return lax.dot_general(a, b, (((1,), (1,)), ((), ())), preferred_element_type=F32)


def _upper_incl(n):
    r = lax.broadcasted_iota(jnp.int32, (n, n), 0)
    c = lax.broadcasted_iota(jnp.int32, (n, n), 1)
    return jnp.where(c >= r, 1.0, 0.0).astype(BF16)


def _stick_block(z, tri, carry, mask):
    nz = -z
    e = jnp.exp(jnp.minimum(z, nz))
    l1m = jnp.minimum(nz, 0.0) - jnp.log(1.0 + e)
    if mask is not None:
        l1m = jnp.where(mask, l1m, 0.0)
    hi = l1m.astype(BF16)
    lo = (l1m - hi.astype(F32)).astype(BF16)
    incl = _dot(tri, hi) + _dot(tri, lo)
    w = jnp.exp(z + incl + carry)
    if mask is not None:
        w = jnp.where(mask, w, 0.0)
    return w, carry + incl[0:1, :]


def _proj_kernel(x_ref, xprev_ref, g_ref, w_ref, wqt_ref, wvt_ref, cw_ref, gconv_ref,
                 k_ref, v_ref, kb_ref, qt_ref, vt_ref, sn_ref, tail_ref, cu_scr, *, tk, scale):
    i = pl.program_id(1)
    tm = x_ref.shape[1]
    a = k_ref.shape[2]
    c = sn_ref.shape[2]
    g = g_ref[...]
    h = _rms(x_ref[0], g).astype(BF16)

    k = _dot(h, w_ref[:, 0:a])
    k_ref[0] = k
    kb_ref[0] = k.astype(BF16)
    v_ref[0] = _dot(h, w_ref[:, a:2 * a])
    qt_ref[0] = (_dot_nt(wqt_ref[...], h) * scale).astype(BF16)
    vt = _dot_nt(wvt_ref[...], h).astype(BF16)
    for j in range(tm // tk):
        vt_ref[0, j] = vt[:, j * tk:(j + 1) * tk]

    gb = _dot(h, w_ref[:, 2 * a:2 * a + c])
    cu = _dot(h, w_ref[:, 2 * a + c:2 * a + 2 * c]) * _dot(h, w_ref[:, 2 * a + 2 * c:2 * a + 3 * c])
    hp = _rms(xprev_ref[0], g).astype(BF16)
    cup = _dot(hp, w_ref[:, 2 * a + c:2 * a + 2 * c]) * _dot(hp, w_ref[:, 2 * a + 2 * c:2 * a + 3 * c])
    cu_scr[0:SUBLANES, :] = jnp.where(i == 0, 0.0, cup)
    cu_scr[SUBLANES:SUBLANES + tm, :] = cu
    y = (cw_ref[0:1, :] * cu_scr[SUBLANES - 2:SUBLANES - 2 + tm, :]
         + cw_ref[1:2, :] * cu_scr[SUBLANES - 1:SUBLANES - 1 + tm, :]
         + cw_ref[2:3, :] * cu)
    sn_ref[0] = _rms(gb * y, gconv_ref[...]).astype(BF16)
    tail_ref[0, 0] = cu[tm - SUBLANES:tm, :]


def _prompt_proj(x, g_pre, w_rest, wqt, wvt, conv_w, g_conv, *, tm, tk):
    b, t, d = x.shape
    a = wqt.shape[0]
    c = conv_w.shape[1]
    nt = t // tm
    rows8 = tm // SUBLANES
    const2 = lambda bi, i: (0, 0)
    out_shape = (
        jax.ShapeDtypeStruct((b, t, a), F32),
        jax.ShapeDtypeStruct((b, t, a), F32),
        jax.ShapeDtypeStruct((b, t, a), BF16),
        jax.ShapeDtypeStruct((b, a, t), BF16),
        jax.ShapeDtypeStruct((b, t // tk, a, tk), BF16),
        jax.ShapeDtypeStruct((b, t, c), BF16),
        jax.ShapeDtypeStruct((b, nt, SUBLANES, c), F32),
    )
    return pl.pallas_call(
        functools.partial(_proj_kernel, tk=tk, scale=HEAD_DIM ** -0.5),
        grid=(b, nt),
        in_specs=[
            pl.BlockSpec((1, tm, d), lambda bi, i: (bi, i, 0)),
            pl.BlockSpec((1, SUBLANES, d), lambda bi, i: (bi, jnp.maximum(i * rows8 - 1, 0), 0)),
            pl.BlockSpec(g_pre.shape, const2),
            pl.BlockSpec(w_rest.shape, const2),
            pl.BlockSpec(wqt.shape, const2),
            pl.BlockSpec(wvt.shape, const2),
            pl.BlockSpec(conv_w.shape, const2),
            pl.BlockSpec(g_conv.shape, const2),
        ],
        out_specs=(
            pl.BlockSpec((1, tm, a), lambda bi, i: (bi, i, 0)),
            pl.BlockSpec((1, tm, a), lambda bi, i: (bi, i, 0)),
            pl.BlockSpec((1, tm, a), lambda bi, i: (bi, i, 0)),
            pl.BlockSpec((1, a, tm), lambda bi, i: (bi, 0, i)),
            pl.BlockSpec((1, tm // tk, a, tk), lambda bi, i: (bi, i, 0, 0)),
            pl.BlockSpec((1, tm, c), lambda bi, i: (bi, i, 0)),
            pl.BlockSpec((1, 1, SUBLANES, c), lambda bi, i: (bi, i, 0, 0)),
        ),
        out_shape=out_shape,
        scratch_shapes=[pltpu.VMEM((tm + SUBLANES, c), F32)],
        compiler_params=pltpu.CompilerParams(
            dimension_semantics=("parallel", "arbitrary"), vmem_limit_bytes=VMEM_LIMIT_BYTES),
        name="prompt_proj",
    )(x, x, g_pre, w_rest, wqt, wvt, conv_w, g_conv)


def _attn_kernel(bias_ref, qt_ref, k_ref, vt_ref, o_ref, q_scr, tri_scr, acc_scr):
    hp = pl.program_id(1)
    i = pl.program_id(2)
    tq = qt_ref.shape[2]
    tk = vt_ref.shape[3]
    pair = qt_ref.shape[1]
    n_heads_blk = pair // HEAD_DIM

    qt = qt_ref[0]
    row = lax.broadcasted_iota(jnp.int32, (pair, tq), 0)
    for hh in range(n_heads_blk):
        in_head = (row >= hh * HEAD_DIM) & (row < (hh + 1) * HEAD_DIM)
        q_scr[hh] = jnp.where(in_head, qt, jnp.zeros_like(qt))
    tri_scr[...] = _upper_incl(tk)
    acc_scr[...] = jnp.zeros_like(acc_scr)
    biases = [bias_ref[n_heads_blk * hp + hh] for hh in range(n_heads_blk)]

    def block(j, carries, mask):
        kb = k_ref[0, pl.ds(pl.multiple_of(j * tk, tk), tk), :]
        vt = vt_ref[0, j]
        new = []
        for hh in range(n_heads_blk):
            z = _dot(kb, q_scr[hh]) + biases[hh]
            w, cnew = _stick_block(z, tri_scr[...], carries[hh], mask)
            acc_scr[hh] += _dot(vt[hh * HEAD_DIM:(hh + 1) * HEAD_DIM, :], w.astype(BF16))
            new.append(cnew)
        return tuple(new)

    kr = lax.broadcasted_iota(jnp.int32, (tk, tq), 0)
    qc = lax.broadcasted_iota(jnp.int32, (tk, tq), 1)
    zero = jnp.zeros((1, tq), F32)
    carries = block(i, (zero,) * n_heads_blk, kr < qc)
    lax.fori_loop(0, i, lambda s, cs: block(i - 1 - s, cs, None), carries)

    out_t = jnp.concatenate([acc_scr[hh] for hh in range(n_heads_blk)], axis=0)
    o_ref[0] = out_t.T


def _prompt_attn(bias, qt, kb, vt, *, tq):
    b, a, t = qt.shape
    nkb, tk = vt.shape[1], vt.shape[3]
    assert tq == tk, "the diagonal block mask assumes square blocks"
    pair = LANES
    return pl.pallas_call(
        _attn_kernel,
        grid=(b, a // pair, t // tq),
        in_specs=[
            pl.BlockSpec(memory_space=pltpu.SMEM),
            pl.BlockSpec((1, pair, tq), lambda bi, hp, i: (bi, hp, i)),
            pl.BlockSpec((1, t, pair), lambda bi, hp, i: (bi, 0, hp)),
            pl.BlockSpec((1, nkb, pair, tk), lambda bi, hp, i: (bi, 0, hp, 0)),
        ],
        out_specs=pl.BlockSpec((1, tq, pair), lambda bi, hp, i: (bi, i, hp)),
        out_shape=jax.ShapeDtypeStruct((b, t, a), F32),
        scratch_shapes=[
            pltpu.VMEM((pair // HEAD_DIM, pair, tq), BF16),
            pltpu.VMEM((tk, tk), BF16),
            pltpu.VMEM((pair // HEAD_DIM, HEAD_DIM, tq), F32),
        ],
        compiler_params=pltpu.CompilerParams(
            dimension_semantics=("parallel", "parallel", "arbitrary"),
            vmem_limit_bytes=VMEM_LIMIT_BYTES),
        name="prompt_attn",
    )(bias, qt, kb, vt)


def _mix_mlp_kernel(x_ref, a_ref, sn_ref, ga_ref, gpm_ref, gpre_ref, gpost_ref,
                    woa_ref, wos_ref, wup_ref, wdn_ref, y_ref, *, ff_chunk):
    an = _rms(a_ref[...], ga_ref[...]).astype(BF16)
    mix = _dot(an, woa_ref[...]) + _dot(sn_ref[...], wos_ref[...])
    x1 = x_ref[...] + _rms(mix, gpm_ref[...])
    hm = _rms(x1, gpre_ref[...]).astype(BF16)
    m = jnp.zeros_like(x1)
    for j in range(wup_ref.shape[1] // ff_chunk):
        up = jnp.maximum(_dot(hm, wup_ref[:, j * ff_chunk:(j + 1) * ff_chunk]), 0.0)
        m = m + _dot((up * up).astype(BF16), wdn_ref[j * ff_chunk:(j + 1) * ff_chunk, :])
    y_ref[...] = x1 + _rms(m, gpost_ref[...])


def _mix_mlp(x, a, sn, g_attn, g_post_mix, g_pre_mlp, g_post_mlp, wo_a, wo_s, w_up, w_dn, *, tm):
    n, d = x.shape
    const = lambda i: (0, 0)
    resident = lambda arr: pl.BlockSpec(arr.shape, const, pipeline_mode=pl.Buffered(1))
    return pl.pallas_call(
        functools.partial(_mix_mlp_kernel, ff_chunk=min(1024, w_up.shape[1])),
        grid=(n // tm,),
        in_specs=[
            pl.BlockSpec((tm, d), lambda i: (i, 0)),
            pl.BlockSpec((tm, a.shape[1]), lambda i: (i, 0)),
            pl.BlockSpec((tm, sn.shape[1]), lambda i: (i, 0)),
            resident(g_attn), resident(g_post_mix), resident(g_pre_mlp), resident(g_post_mlp),
            resident(wo_a), resident(wo_s), resident(w_up), resident(w_dn),
        ],
        out_specs=pl.BlockSpec((tm, d), lambda i: (i, 0)),
        out_shape=jax.ShapeDtypeStruct((n, d), F32),
        compiler_params=pltpu.CompilerParams(
            dimension_semantics=("parallel",), vmem_limit_bytes=VMEM_LIMIT_BYTES),
        name="mix_mlp",
    )(x, a, sn, g_attn, g_post_mix, g_pre_mlp, g_post_mlp, wo_a, wo_s, w_up, w_dn)


def _sample_proj_kernel(x_ref, st0_ref, st1_ref, g_ref, w_ref, cw_ref, gconv_ref,
                        q_ref, k_ref, v_ref, cu_ref, sn_ref, *, scale):
    a = k_ref.shape[1]
    c = sn_ref.shape[1]
    h = _rms(x_ref[...], g_ref[...]).astype(BF16)
    q_ref[...] = _dot(h, w_ref[:, 0:a]) * scale
    k_ref[...] = _dot(h, w_ref[:, a:2 * a])
    v_ref[...] = _dot(h, w_ref[:, 2 * a:3 * a])
    gb = _dot(h, w_ref[:, 3 * a:3 * a + c])
    cu = _dot(h, w_ref[:, 3 * a + c:3 * a + 2 * c]) * _dot(h, w_ref[:, 3 * a + 2 * c:3 * a + 3 * c])
    cu_ref[...] = cu
    y = cw_ref[0:1, :] * st0_ref[...] + cw_ref[1:2, :] * st1_ref[...] + cw_ref[2:3, :] * cu
    sn_ref[...] = _rms(gb * y, gconv_ref[...]).astype(BF16)


def _sample_proj(x, st0, st1, g_pre, w_in, conv_w, g_conv, *, a):
    n = x.shape[0]
    c = conv_w.shape[1]
    return pl.pallas_call(
        functools.partial(_sample_proj_kernel, scale=HEAD_DIM ** -0.5),
        out_shape=(
            jax.ShapeDtypeStruct((n, a), F32), jax.ShapeDtypeStruct((n, a), F32),
            jax.ShapeDtypeStruct((n, a), F32), jax.ShapeDtypeStruct((n, c), F32),
            jax.ShapeDtypeStruct((n, c), BF16),
        ),
        compiler_params=pltpu.CompilerParams(vmem_limit_bytes=VMEM_LIMIT_BYTES),
        name="sample_proj",
    )(x, st0, st1, g_pre, w_in, conv_w, g_conv)


def _sample_attn_kernel(pt_ref, q_ref, knew_ref, vnew_ref, bias_ref, *rest, n_pages_step):
    k_refs = rest[:n_pages_step]
    v_refs = rest[n_pages_step:2 * n_pages_step]
    o_ref = rest[2 * n_pages_step]
    qbd_scr, tri_scr, exp_scr, carry_scr, acc_scr = rest[2 * n_pages_step + 1:]
    del pt_ref
    cstep = pl.program_id(1)
    page = k_refs[0].shape[1]
    a = q_ref.shape[2]
    bias = bias_ref[...]

    def expand(w):
        return _dot(w.astype(BF16), exp_scr[...])

    @pl.when(cstep == 0)
    def _():
        r = lax.broadcasted_iota(jnp.int32, (LANES, a), 0)
        ch = lax.broadcasted_iota(jnp.int32, (LANES, a), 1)
        sel = (ch >= r * HEAD_DIM) & (ch < (r + 1) * HEAD_DIM)
        qbd_scr[...] = jnp.where(sel, q_ref[0], 0.0).astype(BF16)
        exp_scr[...] = jnp.where(sel, 1.0, 0.0).astype(BF16)
        tri_scr[...] = _upper_incl(page)
        tri8 = _upper_incl(SUBLANES)
        rows = lax.broadcasted_iota(jnp.int32, (SUBLANES, LANES), 0)
        k_pos = rows
        q_pos = jnp.zeros_like(rows)
        visible = (k_pos < q_pos) & (rows < knew_ref.shape[1])
        z = _dot_nt(jnp.broadcast_to(knew_ref[0], (SUBLANES, a)).astype(BF16), qbd_scr[...]) + bias
        w, c0 = _stick_block(z, tri8, jnp.zeros((1, LANES), F32), visible)
        carry_scr[...] = c0
        acc_scr[...] = expand(w) * vnew_ref[0]

    carry = carry_scr[...]
    acc = acc_scr[...]
    for p in reversed(range(n_pages_step)):
        z = _dot_nt(k_refs[p][0].astype(BF16), qbd_scr[...]) + bias
        w, carry = _stick_block(z, tri_scr[...], carry, None)
        wv = expand(w) * v_refs[p][0]
        acc = acc + jnp.sum(wv.reshape(page // SUBLANES, SUBLANES, a), axis=0)
    carry_scr[...] = carry
    acc_scr[...] = acc

    @pl.when(cstep == pl.num_programs(1) - 1)
    def _():
        o_ref[0] = jnp.sum(acc, axis=0, keepdims=True)


def _sample_attn(page_table, q, k_new, v_new, bias_row, cache_k, cache_v, *, n_pages_step):
    n, a = q.shape
    n_pages = page_table.shape[1]
    page = cache_k.shape[1]
    n_steps = n_pages // n_pages_step

    def page_map(bi, c, pt, *, p):
        return (pt[bi * n_pages + (n_steps - 1 - c) * n_pages_step + p], 0, 0)

    page_specs = [pl.BlockSpec((1, page, a), functools.partial(page_map, p=p))
                  for p in range(n_pages_step)]
    row_spec = pl.BlockSpec((1, 1, a), lambda bi, c, pt: (bi, 0, 0))
    grid_spec = pltpu.PrefetchScalarGridSpec(
        num_scalar_prefetch=1,
        grid=(n, n_steps),
        in_specs=[row_spec, row_spec, row_spec,
                  pl.BlockSpec((1, LANES), lambda bi, c, pt: (0, 0))] + page_specs + page_specs,
        out_specs=pl.BlockSpec((1, 1, a), lambda bi, c, pt: (bi, 0, 0)),
        scratch_shapes=[
            pltpu.VMEM((LANES, a), BF16),
            pltpu.VMEM((page, page), BF16),
            pltpu.VMEM((LANES, a), BF16),
            pltpu.VMEM((1, LANES), F32),
            pltpu.VMEM((SUBLANES, a), F32),
        ],
    )
    out = pl.pallas_call(
        functools.partial(_sample_attn_kernel, n_pages_step=n_pages_step),
        grid_spec=grid_spec,
        out_shape=jax.ShapeDtypeStruct((n, 1, a), F32),
        compiler_params=pltpu.CompilerParams(
            dimension_semantics=("parallel", "arbitrary"), vmem_limit_bytes=VMEM_LIMIT_BYTES),
        name="sample_attn",
    )(page_table.reshape(-1), q[:, None, :], k_new[:, None, :], v_new[:, None, :], bias_row,
      *([cache_k] * n_pages_step), *([cache_v] * n_pages_step))
    return out[:, 0, :]


def _pick_tile(n, want):
    t = min(n, want)
    while n % t:
        t //= 2
    return t


def _layer_step(xp, xs, cache_k, cache_v, state_conv, page_table,
                g_pre_mix, g_post_mix, g_pre_mlp, g_post_mlp,
                w_in, attn_bias, conv_w, g_attn, g_conv, w_out, w_up, w_down):
    b, t, d = xp.shape
    n_heads = attn_bias.shape[0]
    a = n_heads * HEAD_DIM
    c = conv_w.shape[1]
    row = lambda v: v.reshape(1, -1)
    g_pre_mix, g_post_mix, g_pre_mlp, g_post_mlp = map(row, (g_pre_mix, g_post_mix, g_pre_mlp, g_post_mlp))
    g_attn, g_conv = row(g_attn), row(g_conv)

    w_in_b = w_in.astype(BF16)
    wqt = w_in_b[:, :a].T
    wvt = w_in_b[:, 2 * a:3 * a].T
    w_rest = w_in_b[:, a:]
    wo_a = w_out[:a].astype(BF16)
    wo_s = w_out[a:].astype(BF16)
    w_up_b = w_up.astype(BF16)
    w_dn_b = w_down.astype(BF16)

    tk = _pick_tile(t, 256)
    tm = _pick_tile(t, 512)
    k_p, v_p, kb, qt, vt, sn_p, tails = _prompt_proj(
        xp, g_pre_mix, w_rest, wqt, wvt, conv_w, g_conv, tm=tm, tk=tk)
    a_p = _prompt_attn(attn_bias, qt, kb, vt, tq=tk)
    y_p = _mix_mlp(xp.reshape(b * t, d), a_p.reshape(b * t, a), sn_p.reshape(b * t, c),
                   g_attn, g_post_mix, g_pre_mlp, g_post_mlp, wo_a, wo_s, w_up_b, w_dn_b,
                   tm=_pick_tile(b * t, 512))
    new_conv_p = tails[:, -1, SUBLANES - (CONV_TAPS - 1):, :]

    n = xs.shape[0]
    xs2 = xs.reshape(n, d)
    q_s, k_s, v_s, cu_s, sn_s = _sample_proj(
        xs2, state_conv[:, 0, :], state_conv[:, 1, :], g_pre_mix, w_in_b, conv_w, g_conv, a=a)
    bias_row = jnp.zeros((1, LANES), F32).at[0, :n_heads].set(attn_bias)
    n_phys, page = cache_k.shape[0], cache_k.shape[1]
    a_s = _sample_attn(page_table, q_s, k_s, v_s, bias_row,
                       cache_k.reshape(n_phys, page, a), cache_v.reshape(n_phys, page, a),
                       n_pages_step=_pick_tile(page_table.shape[1], 8))
    y_s = _mix_mlp(xs2, a_s, sn_s, g_attn, g_post_mix, g_pre_mlp, g_post_mlp,
                   wo_a, wo_s, w_up_b, w_dn_b, tm=_pick_tile(n, 512))
    new_conv_s = jnp.stack([state_conv[:, 1, :], cu_s], axis=1)

    return (y_p.reshape(b, t, d), y_s.reshape(n, 1, d),
            k_p.reshape(b, t, n_heads, HEAD_DIM), v_p.reshape(b, t, n_heads, HEAD_DIM), new_conv_p,
            k_s.reshape(n, 1, n_heads, HEAD_DIM), v_s.reshape(n, 1, n_heads, HEAD_DIM), new_conv_s)


def kernel(x_prompt, x_sample, cache_k, cache_v, state_conv, page_table, norm_pre_mix, norm_post_mix,
           norm_pre_mlp, norm_post_mlp, w_in, attn_logit_bias, conv_w, norm_attn_out, norm_conv_out,
           w_out, w_up, w_down):
    depth = w_in.shape[0]
    assert x_sample.shape[1] == 1, "the sample path handles one new token per sequence"
    xp, xs = x_prompt, x_sample
    outs = []
    for l in range(depth):
        res = _layer_step(
            xp, xs, cache_k[l], cache_v[l], state_conv[l], page_table,
            norm_pre_mix[l], norm_post_mix[l], norm_pre_mlp[l], norm_post_mlp[l],
            w_in[l], attn_logit_bias[l], conv_w[l], norm_attn_out[l], norm_conv_out[l],
            w_out[l], w_up[l], w_down[l])
        xp, xs = res[0], res[1]
        outs.append(res[2:])
    stacked = tuple(jnp.stack([o[j] for o in outs]) for j in range(6))
    return (xp, xs) + stacked
```

```python
import functools

import jax
import jax.numpy as jnp
from jax import lax
from jax.experimental import pallas as pl
from jax.experimental.pallas import tpu as pltpu

HEAD_DIM = 64
CONV_TAPS = 3
EPS = 1e-6
SUBLANES = 8
LANES = 128
VMEM_LIMIT_BYTES = 56 * 1024 * 1024
LOG2E = 1.4426950408889634
ATTN_SPAN_WIDTHS = (8, 2, 1)

F32 = jnp.float32
BF16 = jnp.bfloat16


def _rms(x, g):
    return x * lax.rsqrt(jnp.mean(x * x, axis=-1, keepdims=True) + EPS) * g


def _dot(a, b):
    return jnp.dot(a, b, preferred_element_type=F32)


def _dot_nt(a, b):
    return lax.dot_general(a, b, (((1,), (1,)), ((), ())), preferred_element_type=F32)


def _upper_incl(n):
    r = lax.broadcasted_iota(jnp.int32, (n, n), 0)
    c = lax.broadcasted_iota(jnp.int32, (n, n), 1)
    return jnp.where(c >= r, 1.0, 0.0).astype(BF16)


def _stick_block(z, tri, carry, mask):
    nz = -z
    e = jnp.exp(jnp.minimum(z, nz))
    l1m = jnp.minimum(nz, 0.0) - jnp.log(1.0 + e)
    if mask is not None:
        l1m = jnp.where(mask, l1m, 0.0)
    hi = l1m.astype(BF16)
    lo = (l1m - hi.astype(F32)).astype(BF16)
    incl = _dot(tri, hi) + _dot(tri, lo)
    w = jnp.exp(z + incl + carry)
    if mask is not None:
        w = jnp.where(mask, w, 0.0)
    return w, carry + incl[0:1, :]


def _neg_log2_one_minus_beta(z):
    sign = jnp.uint32(0x80000000)
    neg_abs = lax.bitcast_convert_type(lax.bitcast_convert_type(z, jnp.uint32) | sign, F32)
    return jnp.maximum(z, 0.0) + jnp.log2(1.0 + jnp.exp2(neg_abs))


def _proj_kernel(x_ref, xprev_ref, g_ref, w_ref, cw_ref, gconv_ref,
                 k_ref, v_ref, qb_ref, kb_ref, vb_ref, sn_ref, tail_ref, cu_scr, *, q_scale):
    i = pl.program_id(1)
    tm = x_ref.shape[1]
    a = k_ref.shape[2]
    c = sn_ref.shape[2]
    g = g_ref[...]
    h = _rms(x_ref[0], g).astype(BF16)

    qb_ref[0] = (_dot(h, w_ref[:, 0:a]) * q_scale).astype(BF16)
    k = _dot(h, w_ref[:, a:2 * a])
    k_ref[0] = k
    kb_ref[0] = k.astype(BF16)
    v = _dot(h, w_ref[:, 2 * a:3 * a])
    v_ref[0] = v
    vb_ref[0] = v.astype(BF16)

    o = 3 * a
    gb = _dot(h, w_ref[:, o:o + c])
    cu = _dot(h, w_ref[:, o + c:o + 2 * c]) * _dot(h, w_ref[:, o + 2 * c:o + 3 * c])
    hp = _rms(xprev_ref[0], g).astype(BF16)
    cup = _dot(hp, w_ref[:, o + c:o + 2 * c]) * _dot(hp, w_ref[:, o + 2 * c:o + 3 * c])
    cu_scr[0:SUBLANES, :] = jnp.where(i == 0, 0.0, cup)
    cu_scr[SUBLANES:SUBLANES + tm, :] = cu
    y = (cw_ref[0:1, :] * cu_scr[SUBLANES - 2:SUBLANES - 2 + tm, :]
         + cw_ref[1:2, :] * cu_scr[SUBLANES - 1:SUBLANES - 1 + tm, :]
         + cw_ref[2:3, :] * cu)
    sn_ref[0] = _rms(gb * y, gconv_ref[...]).astype(BF16)
    tail_ref[0, 0] = cu[tm - SUBLANES:tm, :]


def _prompt_proj(x, g_pre, w_in, conv_w, g_conv, *, a, tm):
    b, t, d = x.shape
    c = conv_w.shape[1]
    nt = t // tm
    rows8 = tm // SUBLANES
    const2 = lambda bi, i: (0, 0)
    tile = lambda width: pl.BlockSpec((1, tm, width), lambda bi, i: (bi, i, 0))
    out_shape = (
        jax.ShapeDtypeStruct((b, t, a), F32),
        jax.ShapeDtypeStruct((b, t, a), F32),
        jax.ShapeDtypeStruct((b, t, a), BF16),
        jax.ShapeDtypeStruct((b, t, a), BF16),
        jax.ShapeDtypeStruct((b, t, a), BF16),
        jax.ShapeDtypeStruct((b, t, c), BF16),
        jax.ShapeDtypeStruct((b, nt, SUBLANES, c), F32),
    )
    return pl.pallas_call(
        functools.partial(_proj_kernel, q_scale=HEAD_DIM ** -0.5 * LOG2E),
        grid=(b, nt),
        in_specs=[
            tile(d),
            pl.BlockSpec((1, SUBLANES, d), lambda bi, i: (bi, jnp.maximum(i * rows8 - 1, 0), 0)),
            pl.BlockSpec(g_pre.shape, const2),
            pl.BlockSpec(w_in.shape, const2),
            pl.BlockSpec(conv_w.shape, const2),
            pl.BlockSpec(g_conv.shape, const2),
        ],
        out_specs=(tile(a), tile(a), tile(a), tile(a), tile(a), tile(c),
                   pl.BlockSpec((1, 1, SUBLANES, c), lambda bi, i: (bi, i, 0, 0))),
        out_shape=out_shape,
        scratch_shapes=[pltpu.VMEM((tm + SUBLANES, c), F32)],
        compiler_params=pltpu.CompilerParams(
            dimension_semantics=("parallel", "arbitrary"), vmem_limit_bytes=VMEM_LIMIT_BYTES),
        name="prompt_proj",
    )(x, x, g_pre, w_in, conv_w, g_conv)


def _attn_kernel(bias_ref, q_ref, k_ref, v_ref, o_ref, q_scr, tri_scr, carry_scr, acc_scr, *, tk, span_widths):
    hp = pl.program_id(1)
    i = pl.program_id(2)
    tq, pair = q_ref.shape[1], q_ref.shape[2]
    n_heads_blk = pair // HEAD_DIM

    q = q_ref[0]
    lane = lax.broadcasted_iota(jnp.int32, (tq, pair), 1)
    for hh in range(n_heads_blk):
        in_head = (lane >= hh * HEAD_DIM) & (lane < (hh + 1) * HEAD_DIM)
        q_scr[hh] = jnp.where(in_head, q, jnp.zeros_like(q))
    r = lax.broadcasted_iota(jnp.int32, (tk, tk), 0)
    cidx = lax.broadcasted_iota(jnp.int32, (tk, tk), 1)
    tri_scr[...] = jnp.where(r >= cidx, 1.0, 0.0).astype(BF16)
    acc_scr[...] = jnp.zeros_like(acc_scr)
    carry_scr[...] = jnp.zeros_like(carry_scr)
    biases = [bias_ref[n_heads_blk * hp + hh] * LOG2E for hh in range(n_heads_blk)]

    def span(j_lo, n_sub, mask):
        ks = pl.ds(pl.multiple_of(j_lo * tk, tk), n_sub * tk)
        kb = k_ref[0, ks, :]
        vb = v_ref[0, ks, :]
        for hh in range(n_heads_blk):
            z = _dot_nt(q_scr[hh], kb) + biases[hh]
            p = _neg_log2_one_minus_beta(z)
            if mask is not None:
                p = jnp.where(mask, p, 0.0)
            pb = p.astype(BF16)
            carry = carry_scr[hh]
            ws = [None] * n_sub
            for s in reversed(range(n_sub)):
                cols = slice(s * tk, (s + 1) * tk)
                incl = _dot(pb[:, cols], tri_scr[...])
                w = jnp.exp2(z[:, cols] - incl - jnp.concatenate([carry] * (tk // LANES), axis=1))
                if mask is not None:
                    w = jnp.where(mask, w, 0.0)
                ws[s] = w.astype(BF16)
                carry = carry + incl[:, 0:1]
            wb = ws[0] if n_sub == 1 else jnp.concatenate(ws, axis=1)
            acc_scr[hh] += _dot(wb, vb)
            carry_scr[hh] = carry

    qr = lax.broadcasted_iota(jnp.int32, (tq, tk), 0)
    kc = lax.broadcasted_iota(jnp.int32, (tq, tk), 1)
    span(i, 1, kc < qr)

    left = i
    for width in span_widths:
        n_steps = left // width

        def body(s, _, hi=left, width=width):
            span(hi - (s + 1) * width, width, None)
            return 0
        lax.fori_loop(0, n_steps, body, 0)
        left = left - n_steps * width

    out = acc_scr[0]
    for hh in range(1, n_heads_blk):
        out = jnp.where(lane >= hh * HEAD_DIM, acc_scr[hh], out)
    o_ref[0] = out


def _prompt_attn(bias, qb, kb, vb, *, tq):
    b, t, a = qb.shape
    tk = tq
    pair = LANES
    return pl.pallas_call(
        functools.partial(_attn_kernel, tk=tk, span_widths=ATTN_SPAN_WIDTHS),
        grid=(b, a // pair, t // tq),
        in_specs=[
            pl.BlockSpec(memory_space=pltpu.SMEM),
            pl.BlockSpec((1, tq, pair), lambda bi, hp, i: (bi, i, hp)),
            pl.BlockSpec((1, t, pair), lambda bi, hp, i: (bi, 0, hp)),
            pl.BlockSpec((1, t, pair), lambda bi, hp, i: (bi, 0, hp)),
        ],
        out_specs=pl.BlockSpec((1, tq, pair), lambda bi, hp, i: (bi, i, hp)),
        out_shape=jax.ShapeDtypeStruct((b, t, a), F32),
        scratch_shapes=[
            pltpu.VMEM((pair // HEAD_DIM, tq, pair), BF16),
            pltpu.VMEM((tk, tk), BF16),
            pltpu.VMEM((pair // HEAD_DIM, tq, LANES), F32),
            pltpu.VMEM((pair // HEAD_DIM, tq, pair), F32),
        ],
        compiler_params=pltpu.CompilerParams(
            dimension_semantics=("parallel", "parallel", "arbitrary"),
            vmem_limit_bytes=VMEM_LIMIT_BYTES),
        name="prompt_attn",
    )(bias, qb, kb, vb)


def _mix_mlp_kernel(x_ref, a_ref, sn_ref, ga_ref, gpm_ref, gpre_ref, gpost_ref,
                    woa_ref, wos_ref, wup_ref, wdn_ref, y_ref, *, ff_chunk):
    an = _rms(a_ref[...], ga_ref[...]).astype(BF16)
    mix = _dot(an, woa_ref[...]) + _dot(sn_ref[...], wos_ref[...])
    x1 = x_ref[...] + _rms(mix, gpm_ref[...])
    hm = _rms(x1, gpre_ref[...]).astype(BF16)
    m = jnp.zeros_like(x1)
    for j in range(wup_ref.shape[1] // ff_chunk):
        up = jnp.maximum(_dot(hm, wup_ref[:, j * ff_chunk:(j + 1) * ff_chunk]), 0.0)
        m = m + _dot((up * up).astype(BF16), wdn_ref[j * ff_chunk:(j + 1) * ff_chunk, :])
    y_ref[...] = x1 + _rms(m, gpost_ref[...])


def _mix_mlp(x, a, sn, g_attn, g_post_mix, g_pre_mlp, g_post_mlp, wo_a, wo_s, w_up, w_dn, *, tm):
    n, d = x.shape
    const = lambda i: (0, 0)
    resident = lambda arr: pl.BlockSpec(arr.shape, const, pipeline_mode=pl.Buffered(1))
    return pl.pallas_call(
        functools.partial(_mix_mlp_kernel, ff_chunk=min(1024, w_up.shape[1])),
        grid=(n // tm,),
        in_specs=[
            pl.BlockSpec((tm, d), lambda i: (i, 0)),
            pl.BlockSpec((tm, a.shape[1]), lambda i: (i, 0)),
            pl.BlockSpec((tm, sn.shape[1]), lambda i: (i, 0)),
            resident(g_attn), resident(g_post_mix), resident(g_pre_mlp), resident(g_post_mlp),
            resident(wo_a), resident(wo_s), resident(w_up), resident(w_dn),
        ],
        out_specs=pl.BlockSpec((tm, d), lambda i: (i, 0)),
        out_shape=jax.ShapeDtypeStruct((n, d), F32),
        compiler_params=pltpu.CompilerParams(
            dimension_semantics=("parallel",), vmem_limit_bytes=VMEM_LIMIT_BYTES),
        name="mix_mlp",
    )(x, a, sn, g_attn, g_post_mix, g_pre_mlp, g_post_mlp, wo_a, wo_s, w_up, w_dn)


def _sample_proj_kernel(x_ref, st0_ref, st1_ref, g_ref, w_ref, cw_ref, gconv_ref,
                        q_ref, k_ref, v_ref, cu_ref, sn_ref, *, scale):
    a = k_ref.shape[1]
    c = sn_ref.shape[1]
    h = _rms(x_ref[...], g_ref[...]).astype(BF16)
    q_ref[...] = _dot(h, w_ref[:, 0:a]) * scale
    k_ref[...] = _dot(h, w_ref[:, a:2 * a])
    v_ref[...] = _dot(h, w_ref[:, 2 * a:3 * a])
    gb = _dot(h, w_ref[:, 3 * a:3 * a + c])
    cu = _dot(h, w_ref[:, 3 * a + c:3 * a + 2 * c]) * _dot(h, w_ref[:, 3 * a + 2 * c:3 * a + 3 * c])
    cu_ref[...] = cu
    y = cw_ref[0:1, :] * st0_ref[...] + cw_ref[1:2, :] * st1_ref[...] + cw_ref[2:3, :] * cu
    sn_ref[...] = _rms(gb * y, gconv_ref[...]).astype(BF16)


def _sample_proj(x, st0, st1, g_pre, w_in, conv_w, g_conv, *, a):
    n = x.shape[0]
    c = conv_w.shape[1]
    return pl.pallas_call(
        functools.partial(_sample_proj_kernel, scale=HEAD_DIM ** -0.5),
        out_shape=(
            jax.ShapeDtypeStruct((n, a), F32), jax.ShapeDtypeStruct((n, a), F32),
            jax.ShapeDtypeStruct((n, a), F32), jax.ShapeDtypeStruct((n, c), F32),
            jax.ShapeDtypeStruct((n, c), BF16),
        ),
        compiler_params=pltpu.CompilerParams(vmem_limit_bytes=VMEM_LIMIT_BYTES),
        name="sample_proj",
    )(x, st0, st1, g_pre, w_in, conv_w, g_conv)


def _sample_attn_kernel(pt_ref, q_ref, knew_ref, vnew_ref, bias_ref, *rest, n_pages_step):
    k_refs = rest[:n_pages_step]
    v_refs = rest[n_pages_step:2 * n_pages_step]
    o_ref = rest[2 * n_pages_step]
    qbd_scr, tri_scr, exp_scr, carry_scr, acc_scr = rest[2 * n_pages_step + 1:]
    del pt_ref
    cstep = pl.program_id(1)
    page = k_refs[0].shape[1]
    a = q_ref.shape[2]
    bias = bias_ref[...]

    def expand(w):
        return _dot(w.astype(BF16), exp_scr[...])

    @pl.when(cstep == 0)
    def _():
        r = lax.broadcasted_iota(jnp.int32, (LANES, a), 0)
        ch = lax.broadcasted_iota(jnp.int32, (LANES, a), 1)
        sel = (ch >= r * HEAD_DIM) & (ch < (r + 1) * HEAD_DIM)
        qbd_scr[...] = jnp.where(sel, q_ref[0], 0.0).astype(BF16)
        exp_scr[...] = jnp.where(sel, 1.0, 0.0).astype(BF16)
        tri_scr[...] = _upper_incl(page)
        tri8 = _upper_incl(SUBLANES)
        rows = lax.broadcasted_iota(jnp.int32, (SUBLANES, LANES), 0)
        k_pos = rows
        q_pos = jnp.zeros_like(rows)
        visible = (k_pos < q_pos) & (rows < knew_ref.shape[1])
        z = _dot_nt(jnp.broadcast_to(knew_ref[0], (SUBLANES, a)).astype(BF16), qbd_scr[...]) + bias
        w, c0 = _stick_block(z, tri8, jnp.zeros((1, LANES), F32), visible)
        carry_scr[...] = c0
        acc_scr[...] = expand(w) * vnew_ref[0]

    carry = carry_scr[...]
    acc = acc_scr[...]
    for p in reversed(range(n_pages_step)):
        z = _dot_nt(k_refs[p][0].astype(BF16), qbd_scr[...]) + bias
        w, carry = _stick_block(z, tri_scr[...], carry, None)
        wv = expand(w) * v_refs[p][0]
        acc = acc + jnp.sum(wv.reshape(page // SUBLANES, SUBLANES, a), axis=0)
    carry_scr[...] = carry
    acc_scr[...] = acc

    @pl.when(cstep == pl.num_programs(1) - 1)
    def _():
        o_ref[0] = jnp.sum(acc, axis=0, keepdims=True)


def _sample_attn(page_table, q, k_new, v_new, bias_row, cache_k, cache_v, *, n_pages_step):
    n, a = q.shape
    n_pages = page_table.shape[1]
    page = cache_k.shape[1]
    n_steps = n_pages // n_pages_step

    def page_map(bi, c, pt, *, p):
        return (pt[bi * n_pages + (n_steps - 1 - c) * n_pages_step + p], 0, 0)

    page_specs = [pl.BlockSpec((1, page, a), functools.partial(page_map, p=p))
                  for p in range(n_pages_step)]
    row_spec = pl.BlockSpec((1, 1, a), lambda bi, c, pt: (bi, 0, 0))
    grid_spec = pltpu.PrefetchScalarGridSpec(
        num_scalar_prefetch=1,
        grid=(n, n_steps),
        in_specs=[row_spec, row_spec, row_spec,
                  pl.BlockSpec((1, LANES), lambda bi, c, pt: (0, 0))] + page_specs + page_specs,
        out_specs=pl.BlockSpec((1, 1, a), lambda bi, c, pt: (bi, 0, 0)),
        scratch_shapes=[
            pltpu.VMEM((LANES, a), BF16),
            pltpu.VMEM((page, page), BF16),
            pltpu.VMEM((LANES, a), BF16),
            pltpu.VMEM((1, LANES), F32),
            pltpu.VMEM((SUBLANES, a), F32),
        ],
    )
    out = pl.pallas_call(
        functools.partial(_sample_attn_kernel, n_pages_step=n_pages_step),
        grid_spec=grid_spec,
        out_shape=jax.ShapeDtypeStruct((n, 1, a), F32),
        compiler_params=pltpu.CompilerParams(
            dimension_semantics=("parallel", "arbitrary"), vmem_limit_bytes=VMEM_LIMIT_BYTES),
        name="sample_attn",
    )(page_table.reshape(-1), q[:, None, :], k_new[:, None, :], v_new[:, None, :], bias_row,
      *([cache_k] * n_pages_step), *([cache_v] * n_pages_step))
    return out[:, 0, :]


def _pick_tile(n, want):
    t = min(n, want)
    while n % t:
        t //= 2
    return t


def _layer_step(xp, xs, cache_k, cache_v, state_conv, page_table,
                g_pre_mix, g_post_mix, g_pre_mlp, g_post_mlp,
                w_in, attn_bias, conv_w, g_attn, g_conv, w_out, w_up, w_down):
    b, t, d = xp.shape
    n_heads = attn_bias.shape[0]
    a = n_heads * HEAD_DIM
    c = conv_w.shape[1]
    row = lambda v: v.reshape(1, -1)
    g_pre_mix, g_post_mix, g_pre_mlp, g_post_mlp = map(row, (g_pre_mix, g_post_mix, g_pre_mlp, g_post_mlp))
    g_attn, g_conv = row(g_attn), row(g_conv)

    w_in_b = w_in.astype(BF16)
    wo_a = w_out[:a].astype(BF16)
    wo_s = w_out[a:].astype(BF16)
    w_up_b = w_up.astype(BF16)
    w_dn_b = w_down.astype(BF16)

    k_p, v_p, qb, kb, vb, sn_p, tails = _prompt_proj(
        xp, g_pre_mix, w_in_b, conv_w, g_conv, a=a, tm=_pick_tile(t, 512))
    a_p = _prompt_attn(attn_bias, qb, kb, vb, tq=_pick_tile(t, 256))
    y_p = _mix_mlp(xp.reshape(b * t, d), a_p.reshape(b * t, a), sn_p.reshape(b * t, c),
                   g_attn, g_post_mix, g_pre_mlp, g_post_mlp, wo_a, wo_s, w_up_b, w_dn_b,
                   tm=_pick_tile(b * t, 512))
    new_conv_p = tails[:, -1, SUBLANES - (CONV_TAPS - 1):, :]

    n = xs.shape[0]
    xs2 = xs.reshape(n, d)
    q_s, k_s, v_s, cu_s, sn_s = _sample_proj(
        xs2, state_conv[:, 0, :], state_conv[:, 1, :], g_pre_mix, w_in_b, conv_w, g_conv, a=a)
    bias_row = jnp.zeros((1, LANES), F32).at[0, :n_heads].set(attn_bias)
    n_phys, page = cache_k.shape[0], cache_k.shape[1]
    a_s = _sample_attn(page_table, q_s, k_s, v_s, bias_row,
                       cache_k.reshape(n_phys, page, a), cache_v.reshape(n_phys, page, a),
                       n_pages_step=_pick_tile(page_table.shape[1], 8))
    y_s = _mix_mlp(xs2, a_s, sn_s, g_attn, g_post_mix, g_pre_mlp, g_post_mlp,
                   wo_a, wo_s, w_up_b, w_dn_b, tm=_pick_tile(n, 512))
    new_conv_s = jnp.stack([state_conv[:, 1, :], cu_s], axis=1)

    return (y_p.reshape(b, t, d), y_s.reshape(n, 1, d),
            k_p.reshape(b, t, n_heads, HEAD_DIM), v_p.reshape(b, t, n_heads, HEAD_DIM), new_conv_p,
            k_s.reshape(n, 1, n_heads, HEAD_DIM), v_s.reshape(n, 1, n_heads, HEAD_DIM), new_conv_s)


def kernel(x_prompt, x_sample, cache_k, cache_v, state_conv, page_table, norm_pre_mix, norm_post_mix,
           norm_pre_mlp, norm_post_mlp, w_in, attn_logit_bias, conv_w, norm_attn_out, norm_conv_out,
           w_out, w_up, w_down):
    depth = w_in.shape[0]
    assert x_sample.shape[1] == 1, "the sample path handles one new token per sequence"
    xp, xs = x_prompt, x_sample
    outs = []
    for l in range(depth):
        res = _layer_step(
            xp, xs, cache_k[l], cache_v[l], state_conv[l], page_table,
            norm_pre_mix[l], norm_post_mix[l], norm_pre_mlp[l], norm_post_mlp[l],
            w_in[l], attn_logit_bias[l], conv_w[l], norm_attn_out[l], norm_conv_out[l],
            w_out[l], w_up[l], w_down[l])
        xp, xs = res[0], res[1]
        outs.append(res[2:])
    stacked = tuple(jnp.stack([o[j] for o in outs]) for j in range(6))
    return (xp, xs) + stacked
```

```python
import functools

import jax
import jax.numpy as jnp
from jax import lax
from jax.experimental import pallas as pl
from jax.experimental.pallas import tpu as pltpu

HEAD_DIM = 64
CONV_TAPS = 3
EPS = 1e-6
SUBLANES = 8
LANES = 128
VMEM_LIMIT_BYTES = 56 * 1024 * 1024
LOG2E = 1.4426950408889634
ATTN_SPAN_WIDTHS = (8, 2, 1)

F32 = jnp.float32
BF16 = jnp.bfloat16


def _rms(x, g):
    return x * lax.rsqrt(jnp.mean(x * x, axis=-1, keepdims=True) + EPS) * g


def _dot(a, b):
    return jnp.dot(a, b, preferred_element_type=F32)


def _dot_nt(a, b):
    return lax.dot_general(a, b, (((1,), (1,)), ((), ())), preferred_element_type=F32)


def _neg_log2_one_minus_beta(z):
    sign = jnp.uint32(0x80000000)
    neg_abs = lax.bitcast_convert_type(lax.bitcast_convert_type(z, jnp.uint32) | sign, F32)
    return jnp.maximum(z, 0.0) + jnp.log2(1.0 + jnp.exp2(neg_abs))


def _proj_kernel(x_ref, xprev_ref, g_ref, w_ref, cw_ref, gconv_ref,
                 k_ref, v_ref, qb_ref, kb_ref, vb_ref, sn_ref, tail_ref, cu_scr, *, q_scale):
    i = pl.program_id(1)
    tm = x_ref.shape[1]
    a = k_ref.shape[2]
    c = sn_ref.shape[2]
    g = g_ref[...]
    h = _rms(x_ref[0], g).astype(BF16)

    qb_ref[0] = (_dot(h, w_ref[:, 0:a]) * q_scale).astype(BF16)
    k = _dot(h, w_ref[:, a:2 * a])
    k_ref[0] = k
    kb_ref[0] = k.astype(BF16)
    v = _dot(h, w_ref[:, 2 * a:3 * a])
    v_ref[0] = v
    vb_ref[0] = v.astype(BF16)

    o = 3 * a
    gb = _dot(h, w_ref[:, o:o + c])
    cu = _dot(h, w_ref[:, o + c:o + 2 * c]) * _dot(h, w_ref[:, o + 2 * c:o + 3 * c])
    hp = _rms(xprev_ref[0], g).astype(BF16)
    cup = _dot(hp, w_ref[:, o + c:o + 2 * c]) * _dot(hp, w_ref[:, o + 2 * c:o + 3 * c])
    cu_scr[0:SUBLANES, :] = jnp.where(i == 0, 0.0, cup)
    cu_scr[SUBLANES:SUBLANES + tm, :] = cu
    y = (cw_ref[0:1, :] * cu_scr[SUBLANES - 2:SUBLANES - 2 + tm, :]
         + cw_ref[1:2, :] * cu_scr[SUBLANES - 1:SUBLANES - 1 + tm, :]
         + cw_ref[2:3, :] * cu)
    sn_ref[0] = _rms(gb * y, gconv_ref[...]).astype(BF16)
    tail_ref[0, 0] = cu[tm - SUBLANES:tm, :]


def _prompt_proj(x, g_pre, w_in, conv_w, g_conv, *, a, tm):
    b, t, d = x.shape
    c = conv_w.shape[1]
    nt = t // tm
    rows8 = tm // SUBLANES
    const2 = lambda bi, i: (0, 0)
    tile = lambda width: pl.BlockSpec((1, tm, width), lambda bi, i: (bi, i, 0))
    out_shape = (
        jax.ShapeDtypeStruct((b, t, a), F32),
        jax.ShapeDtypeStruct((b, t, a), F32),
        jax.ShapeDtypeStruct((b, t, a), BF16),
        jax.ShapeDtypeStruct((b, t, a), BF16),
        jax.ShapeDtypeStruct((b, t, a), BF16),
        jax.ShapeDtypeStruct((b, t, c), BF16),
        jax.ShapeDtypeStruct((b, nt, SUBLANES, c), F32),
    )
    return pl.pallas_call(
        functools.partial(_proj_kernel, q_scale=HEAD_DIM ** -0.5 * LOG2E),
        grid=(b, nt),
        in_specs=[
            tile(d),
            pl.BlockSpec((1, SUBLANES, d), lambda bi, i: (bi, jnp.maximum(i * rows8 - 1, 0), 0)),
            pl.BlockSpec(g_pre.shape, const2),
            pl.BlockSpec(w_in.shape, const2),
            pl.BlockSpec(conv_w.shape, const2),
            pl.BlockSpec(g_conv.shape, const2),
        ],
        out_specs=(tile(a), tile(a), tile(a), tile(a), tile(a), tile(c),
                   pl.BlockSpec((1, 1, SUBLANES, c), lambda bi, i: (bi, i, 0, 0))),
        out_shape=out_shape,
        scratch_shapes=[pltpu.VMEM((tm + SUBLANES, c), F32)],
        compiler_params=pltpu.CompilerParams(
            dimension_semantics=("parallel", "arbitrary"), vmem_limit_bytes=VMEM_LIMIT_BYTES),
        name="prompt_proj",
    )(x, x, g_pre, w_in, conv_w, g_conv)


def _attn_kernel(bias_ref, q_ref, k_ref, v_ref, o_ref, q_scr, tri_scr, carry_scr, acc_scr, *, tk, span_widths):
    hp = pl.program_id(1)
    i = pl.program_id(2)
    tq, pair = q_ref.shape[1], q_ref.shape[2]
    n_heads_blk = pair // HEAD_DIM

    q = q_ref[0]
    lane = lax.broadcasted_iota(jnp.int32, (tq, pair), 1)
    for hh in range(n_heads_blk):
        in_head = (lane >= hh * HEAD_DIM) & (lane < (hh + 1) * HEAD_DIM)
        q_scr[hh] = jnp.where(in_head, q, jnp.zeros_like(q))
    r = lax.broadcasted_iota(jnp.int32, (tk, tk), 0)
    cidx = lax.broadcasted_iota(jnp.int32, (tk, tk), 1)
    tri_scr[...] = jnp.where(r >= cidx, 1.0, 0.0).astype(BF16)
    acc_scr[...] = jnp.zeros_like(acc_scr)
    carry_scr[...] = jnp.zeros_like(carry_scr)
    biases = [bias_ref[n_heads_blk * hp + hh] * LOG2E for hh in range(n_heads_blk)]

    def span(j_lo, n_sub, mask):
        ks = pl.ds(pl.multiple_of(j_lo * tk, tk), n_sub * tk)
        kb = k_ref[0, ks, :]
        vb = v_ref[0, ks, :]
        for hh in range(n_heads_blk):
            z = _dot_nt(q_scr[hh], kb) + biases[hh]
            p = _neg_log2_one_minus_beta(z)
            if mask is not None:
                p = jnp.where(mask, p, 0.0)
            pb = p.astype(BF16)
            carry = carry_scr[hh]
            ws = [None] * n_sub
            for s in reversed(range(n_sub)):
                cols = slice(s * tk, (s + 1) * tk)
                incl = _dot(pb[:, cols], tri_scr[...])
                w = jnp.exp2(z[:, cols] - incl - jnp.concatenate([carry] * (tk // LANES), axis=1))
                if mask is not None:
                    w = jnp.where(mask, w, 0.0)
                ws[s] = w.astype(BF16)
                carry = carry + incl[:, 0:1]
            wb = ws[0] if n_sub == 1 else jnp.concatenate(ws, axis=1)
            acc_scr[hh] += _dot(wb, vb)
            carry_scr[hh] = carry

    qr = lax.broadcasted_iota(jnp.int32, (tq, tk), 0)
    kc = lax.broadcasted_iota(jnp.int32, (tq, tk), 1)
    span(i, 1, kc < qr)

    left = i
    for width in span_widths:
        n_steps = left // width

        def body(s, _, hi=left, width=width):
            span(hi - (s + 1) * width, width, None)
            return 0
        lax.fori_loop(0, n_steps, body, 0)
        left = left - n_steps * width

    out = acc_scr[0]
    for hh in range(1, n_heads_blk):
        out = jnp.where(lane >= hh * HEAD_DIM, acc_scr[hh], out)
    o_ref[0] = out


def _prompt_attn(bias, qb, kb, vb, *, tq):
    b, t, a = qb.shape
    tk = tq
    pair = LANES
    return pl.pallas_call(
        functools.partial(_attn_kernel, tk=tk, span_widths=ATTN_SPAN_WIDTHS),
        grid=(b, a // pair, t // tq),
        in_specs=[
            pl.BlockSpec(memory_space=pltpu.SMEM),
            pl.BlockSpec((1, tq, pair), lambda bi, hp, i: (bi, i, hp)),
            pl.BlockSpec((1, t, pair), lambda bi, hp, i: (bi, 0, hp)),
            pl.BlockSpec((1, t, pair), lambda bi, hp, i: (bi, 0, hp)),
        ],
        out_specs=pl.BlockSpec((1, tq, pair), lambda bi, hp, i: (bi, i, hp)),
        out_shape=jax.ShapeDtypeStruct((b, t, a), F32),
        scratch_shapes=[
            pltpu.VMEM((pair // HEAD_DIM, tq, pair), BF16),
            pltpu.VMEM((tk, tk), BF16),
            pltpu.VMEM((pair // HEAD_DIM, tq, LANES), F32),
            pltpu.VMEM((pair // HEAD_DIM, tq, pair), F32),
        ],
        compiler_params=pltpu.CompilerParams(
            dimension_semantics=("parallel", "parallel", "arbitrary"),
            vmem_limit_bytes=VMEM_LIMIT_BYTES),
        name="prompt_attn",
    )(bias, qb, kb, vb)


def _mix_mlp_kernel(x_ref, a_ref, sn_ref, ga_ref, gpm_ref, gpre_ref, gpost_ref,
                    woa_ref, wos_ref, wup_ref, wdn_ref, y_ref, *, ff_chunk):
    an = _rms(a_ref[...], ga_ref[...]).astype(BF16)
    mix = _dot(an, woa_ref[...]) + _dot(sn_ref[...], wos_ref[...])
    x1 = x_ref[...] + _rms(mix, gpm_ref[...])
    hm = _rms(x1, gpre_ref[...]).astype(BF16)
    m = jnp.zeros_like(x1)
    for j in range(wup_ref.shape[1] // ff_chunk):
        up = jnp.maximum(_dot(hm, wup_ref[:, j * ff_chunk:(j + 1) * ff_chunk]), 0.0)
        m = m + _dot((up * up).astype(BF16), wdn_ref[j * ff_chunk:(j + 1) * ff_chunk, :])
    y_ref[...] = x1 + _rms(m, gpost_ref[...])


def _mix_mlp(x, a, sn, g_attn, g_post_mix, g_pre_mlp, g_post_mlp, wo_a, wo_s, w_up, w_dn, *, tm):
    n, d = x.shape
    const = lambda i: (0, 0)
    resident = lambda arr: pl.BlockSpec(arr.shape, const, pipeline_mode=pl.Buffered(1))
    return pl.pallas_call(
        functools.partial(_mix_mlp_kernel, ff_chunk=min(1024, w_up.shape[1])),
        grid=(n // tm,),
        in_specs=[
            pl.BlockSpec((tm, d), lambda i: (i, 0)),
            pl.BlockSpec((tm, a.shape[1]), lambda i: (i, 0)),
            pl.BlockSpec((tm, sn.shape[1]), lambda i: (i, 0)),
            resident(g_attn), resident(g_post_mix), resident(g_pre_mlp), resident(g_post_mlp),
            resident(wo_a), resident(wo_s), resident(w_up), resident(w_dn),
        ],
        out_specs=pl.BlockSpec((tm, d), lambda i: (i, 0)),
        out_shape=jax.ShapeDtypeStruct((n, d), F32),
        compiler_params=pltpu.CompilerParams(
            dimension_semantics=("parallel",), vmem_limit_bytes=VMEM_LIMIT_BYTES),
        name="mix_mlp",
    )(x, a, sn, g_attn, g_post_mix, g_pre_mlp, g_post_mlp, wo_a, wo_s, w_up, w_dn)


def _sample_proj_kernel(x_ref, st0_ref, st1_ref, g_ref, w_ref, cw_ref, gconv_ref,
                        q_ref, k_ref, v_ref, cu_ref, sn_ref, *, scale):
    a = k_ref.shape[1]
    c = sn_ref.shape[1]
    h = _rms(x_ref[...], g_ref[...]).astype(BF16)
    q_ref[...] = _dot(h, w_ref[:, 0:a]) * scale
    k_ref[...] = _dot(h, w_ref[:, a:2 * a])
    v_ref[...] = _dot(h, w_ref[:, 2 * a:3 * a])
    gb = _dot(h, w_ref[:, 3 * a:3 * a + c])
    cu = _dot(h, w_ref[:, 3 * a + c:3 * a + 2 * c]) * _dot(h, w_ref[:, 3 * a + 2 * c:3 * a + 3 * c])
    cu_ref[...] = cu
    y = cw_ref[0:1, :] * st0_ref[...] + cw_ref[1:2, :] * st1_ref[...] + cw_ref[2:3, :] * cu
    sn_ref[...] = _rms(gb * y, gconv_ref[...]).astype(BF16)


def _sample_proj(x, st0, st1, g_pre, w_in, conv_w, g_conv, *, a):
    n = x.shape[0]
    c = conv_w.shape[1]
    return pl.pallas_call(
        functools.partial(_sample_proj_kernel, scale=HEAD_DIM ** -0.5 * LOG2E),
        out_shape=(
            jax.ShapeDtypeStruct((n, a), F32), jax.ShapeDtypeStruct((n, a), F32),
            jax.ShapeDtypeStruct((n, a), F32), jax.ShapeDtypeStruct((n, c), F32),
            jax.ShapeDtypeStruct((n, c), BF16),
        ),
        compiler_params=pltpu.CompilerParams(vmem_limit_bytes=VMEM_LIMIT_BYTES),
        name="sample_proj",
    )(x, st0, st1, g_pre, w_in, conv_w, g_conv)


def _paged_attn_kernel(pt_ref, q_ref, knew_ref, vnew_ref, bias_ref, *rest):
    n_pg = SUBLANES
    k_refs = rest[:n_pg]
    v_refs = rest[n_pg:2 * n_pg]
    o_ref = rest[2 * n_pg]
    qbd_scr, tri_scr, carry_scr, acc_scr = rest[2 * n_pg + 1:]
    del pt_ref
    cstep = pl.program_id(1)
    n_heads, hd, page = k_refs[0].shape[1:]
    a = n_heads * hd
    bias = bias_ref[...]

    def weights(z, carry, mask):
        p = _neg_log2_one_minus_beta(z)
        if mask is not None:
            p = jnp.where(mask, p, 0.0)
        incl = _dot(p.astype(BF16), tri_scr[...])
        w = jnp.exp2(z - incl - carry)
        if mask is not None:
            w = jnp.where(mask, w, 0.0)
        return w, incl[:, 0:1]

    @pl.when(cstep == 0)
    def _():
        sub = lax.broadcasted_iota(jnp.int32, (n_heads, a), 0)
        lane = lax.broadcasted_iota(jnp.int32, (n_heads, a), 1)
        q_all = jnp.concatenate([q_ref[0]] * n_heads, axis=1)
        qbd_scr[...] = jnp.where(lane // hd == sub, q_all, 0.0).astype(BF16)
        r = lax.broadcasted_iota(jnp.int32, (page, page), 0)
        c = lax.broadcasted_iota(jnp.int32, (page, page), 1)
        tri_scr[...] = jnp.where(r >= c, 1.0, 0.0).astype(BF16)
        n_new = knew_ref.shape[2]
        slot = lax.broadcasted_iota(jnp.int32, (n_heads, page), 1)
        q_off = 0
        visible = (slot < q_off) & (slot < n_new)
        k_new = jnp.broadcast_to(knew_ref[0][:, 0:1], (a, page)).astype(BF16)
        v_new = jnp.broadcast_to(vnew_ref[0][:, 0:1], (a, page)).astype(BF16)
        z = _dot(qbd_scr[...], k_new) + bias
        w, tot = weights(z, jnp.zeros((n_heads, page), F32), visible)
        carry_scr[...] = jnp.broadcast_to(tot, (n_heads, page))
        acc_scr[...] = _dot_nt(w.astype(BF16), v_new)

    qbd = qbd_scr[...]
    zs = [_dot(qbd, k_refs[r][0].reshape(a, page).astype(BF16)) + bias for r in range(n_pg)]
    z = jnp.concatenate(zs, axis=0)
    p = _neg_log2_one_minus_beta(z)
    incl = _dot(p.astype(BF16), tri_scr[...])
    carry = carry_scr[...]
    acc = acc_scr[...]
    for r in reversed(range(n_pg)):
        rows = slice(r * n_heads, (r + 1) * n_heads)
        w = jnp.exp2(zs[r] - incl[rows] - carry)
        acc = acc + _dot_nt(w.astype(BF16), v_refs[r][0].reshape(a, page).astype(BF16))
        carry = carry + incl[rows, 0:1]
    carry_scr[...] = carry
    acc_scr[...] = acc

    @pl.when(cstep == pl.num_programs(1) - 1)
    def _():
        sub = lax.broadcasted_iota(jnp.int32, (n_heads, a), 0)
        lane = lax.broadcasted_iota(jnp.int32, (n_heads, a), 1)
        o_ref[0] = jnp.sum(jnp.where(lane // hd == sub, acc, 0.0), axis=0, keepdims=True)


def _paged_attn(page_table, q, k_new, v_new, bias, cache_kt, cache_vt):
    n, n_heads, hd = q.shape
    n_pages = page_table.shape[1]
    page = cache_kt.shape[3]
    a = n_heads * hd
    assert n_pages % SUBLANES == 0 and hd == HEAD_DIM
    n_steps = n_pages // SUBLANES
    bias_tile = jnp.broadcast_to(bias[:, None], (n_heads, page))

    def page_map(bi, c, pt, *, r):
        return (pt[bi * n_pages + (n_steps - 1 - c) * SUBLANES + r], 0, 0, 0)

    page_specs = [pl.BlockSpec((1, n_heads, hd, page), functools.partial(page_map, r=r))
                  for r in range(SUBLANES)]
    new_spec = pl.BlockSpec((1,) + k_new.shape[1:], lambda bi, c, pt: (bi, 0, 0))
    grid_spec = pltpu.PrefetchScalarGridSpec(
        num_scalar_prefetch=1,
        grid=(n, n_steps),
        in_specs=[pl.BlockSpec((1, n_heads, hd), lambda bi, c, pt: (bi, 0, 0)), new_spec, new_spec,
                  pl.BlockSpec((n_heads, page), lambda bi, c, pt: (0, 0))] + page_specs + page_specs,
        out_specs=pl.BlockSpec((1, 1, a), lambda bi, c, pt: (bi, 0, 0)),
        scratch_shapes=[
            pltpu.VMEM((n_heads, a), BF16),
            pltpu.VMEM((page, page), BF16),
            pltpu.VMEM((n_heads, page), F32),
            pltpu.VMEM((n_heads, a), F32),
        ],
    )
    out = pl.pallas_call(
        _paged_attn_kernel,
        grid_spec=grid_spec,
        out_shape=jax.ShapeDtypeStruct((n, 1, a), F32),
        compiler_params=pltpu.CompilerParams(
            dimension_semantics=("parallel", "arbitrary"), vmem_limit_bytes=VMEM_LIMIT_BYTES),
        name="sample_attn",
    )(page_table.reshape(-1), q, k_new, v_new, bias_tile,
      *([cache_kt] * SUBLANES), *([cache_vt] * SUBLANES))
    return out[:, 0, :]


def _pick_tile(n, want):
    t = min(n, want)
    while n % t:
        t //= 2
    return t


def _layer_step(xp, xs, cache_k, cache_v, state_conv, page_table,
                g_pre_mix, g_post_mix, g_pre_mlp, g_post_mlp,
                w_in, attn_bias, conv_w, g_attn, g_conv, w_out, w_up, w_down):
    b, t, d = xp.shape
    n_heads = attn_bias.shape[0]
    a = n_heads * HEAD_DIM
    c = conv_w.shape[1]
    row = lambda v: v.reshape(1, -1)
    g_pre_mix, g_post_mix, g_pre_mlp, g_post_mlp = map(row, (g_pre_mix, g_post_mix, g_pre_mlp, g_post_mlp))
    g_attn, g_conv = row(g_attn), row(g_conv)

    w_in_b = w_in.astype(BF16)
    wo_a = w_out[:a].astype(BF16)
    wo_s = w_out[a:].astype(BF16)
    w_up_b = w_up.astype(BF16)
    w_dn_b = w_down.astype(BF16)

    k_p, v_p, qb, kb, vb, sn_p, tails = _prompt_proj(
        xp, g_pre_mix, w_in_b, conv_w, g_conv, a=a, tm=_pick_tile(t, 512))
    a_p = _prompt_attn(attn_bias, qb, kb, vb, tq=_pick_tile(t, 256))
    y_p = _mix_mlp(xp.reshape(b * t, d), a_p.reshape(b * t, a), sn_p.reshape(b * t, c),
                   g_attn, g_post_mix, g_pre_mlp, g_post_mlp, wo_a, wo_s, w_up_b, w_dn_b,
                   tm=_pick_tile(b * t, 512))
    new_conv_p = tails[:, -1, SUBLANES - (CONV_TAPS - 1):, :]

    n = xs.shape[0]
    xs2 = xs.reshape(n, d)
    q_s, k_s, v_s, cu_s, sn_s = _sample_proj(
        xs2, state_conv[:, 0, :], state_conv[:, 1, :], g_pre_mix, w_in_b, conv_w, g_conv, a=a)
    to_keys_minor = lambda cache: jnp.transpose(cache, (0, 2, 3, 1))
    a_s = _paged_attn(page_table, q_s.reshape(n, n_heads, HEAD_DIM), k_s[:, :, None], v_s[:, :, None],
                      attn_bias * LOG2E, to_keys_minor(cache_k), to_keys_minor(cache_v))
    y_s = _mix_mlp(xs2, a_s, sn_s, g_attn, g_post_mix, g_pre_mlp, g_post_mlp,
                   wo_a, wo_s, w_up_b, w_dn_b, tm=_pick_tile(n, 512))
    new_conv_s = jnp.stack([state_conv[:, 1, :], cu_s], axis=1)

    return (y_p.reshape(b, t, d), y_s.reshape(n, 1, d),
            k_p.reshape(b, t, n_heads, HEAD_DIM), v_p.reshape(b, t, n_heads, HEAD_DIM), new_conv_p,
            k_s.reshape(n, 1, n_heads, HEAD_DIM), v_s.reshape(n, 1, n_heads, HEAD_DIM), new_conv_s)


def kernel(x_prompt, x_sample, cache_k, cache_v, state_conv, page_table, norm_pre_mix, norm_post_mix,
           norm_pre_mlp, norm_post_mlp, w_in, attn_logit_bias, conv_w, norm_attn_out, norm_conv_out,
           w_out, w_up, w_down):
    depth = w_in.shape[0]
    assert x_sample.shape[1] == 1, "the sample path handles one new token per sequence"
    xp, xs = x_prompt, x_sample
    outs = []
    for l in range(depth):
        res = _layer_step(
            xp, xs, cache_k[l], cache_v[l], state_conv[l], page_table,
            norm_pre_mix[l], norm_post_mix[l], norm_pre_mlp[l], norm_post_mlp[l],
            w_in[l], attn_logit_bias[l], conv_w[l], norm_attn_out[l], norm_conv_out[l],
            w_out[l], w_up[l], w_down[l])
        xp, xs = res[0], res[1]
        outs.append(res[2:])
    stacked = tuple(jnp.stack([o[j] for o in outs]) for j in range(6))
    return (xp, xs) + stacked
```

```python
import functools

import jax
import jax.numpy as jnp
from jax import lax
from jax.experimental import pallas as pl
from jax.experimental.pallas import tpu as pltpu

HEAD_DIM = 64
CONV_TAPS = 3
EPS = 1e-6
SUBLANES = 8
LANES = 128
VMEM_LIMIT_BYTES = 56 * 1024 * 1024
LOG2E = 1.4426950408889634
EXP2_SAFE = 126.0
ATTN_SPAN_WIDTHS = (16, 8, 4, 2, 1)
SAMPLE_PAGES_PER_STEP = 16

F32 = jnp.float32
BF16 = jnp.bfloat16


def _rms(x, g):
    return x * lax.rsqrt(jnp.mean(x * x, axis=-1, keepdims=True) + EPS) * g


def _dot(a, b):
    return jnp.dot(a, b, preferred_element_type=F32)


def _dot_nt(a, b):
    return lax.dot_general(a, b, (((1,), (1,)), ((), ())), preferred_element_type=F32)


def _neg_log2_one_minus_beta(z):
    return jnp.maximum(z, jnp.log2(1.0 + jnp.exp2(jnp.minimum(z, EXP2_SAFE))))


def _proj_kernel(x_ref, xprev_ref, g_ref, w_ref, cw_ref, gconv_ref,
                 k_ref, v_ref, qb_ref, kb_ref, vb_ref, sn_ref, tail_ref, cu_scr, *, q_scale):
    i = pl.program_id(1)
    tm = x_ref.shape[1]
    a = k_ref.shape[2]
    c = sn_ref.shape[2]
    g = g_ref[...]
    h = _rms(x_ref[0], g).astype(BF16)

    qb_ref[0] = (_dot(h, w_ref[:, 0:a]) * q_scale).astype(BF16)
    k = _dot(h, w_ref[:, a:2 * a])
    k_ref[0] = k
    kb_ref[0] = k.astype(BF16)
    v = _dot(h, w_ref[:, 2 * a:3 * a])
    v_ref[0] = v
    vb_ref[0] = v.astype(BF16)

    o = 3 * a
    gb = _dot(h, w_ref[:, o:o + c])
    cu = _dot(h, w_ref[:, o + c:o + 2 * c]) * _dot(h, w_ref[:, o + 2 * c:o + 3 * c])
    hp = _rms(xprev_ref[0], g).astype(BF16)
    cup = _dot(hp, w_ref[:, o + c:o + 2 * c]) * _dot(hp, w_ref[:, o + 2 * c:o + 3 * c])
    cu_scr[0:SUBLANES, :] = jnp.where(i == 0, 0.0, cup)
    cu_scr[SUBLANES:SUBLANES + tm, :] = cu
    y = (cw_ref[0:1, :] * cu_scr[SUBLANES - 2:SUBLANES - 2 + tm, :]
         + cw_ref[1:2, :] * cu_scr[SUBLANES - 1:SUBLANES - 1 + tm, :]
         + cw_ref[2:3, :] * cu)
    sn_ref[0] = _rms(gb * y, gconv_ref[...]).astype(BF16)
    tail_ref[0, 0] = cu[tm - SUBLANES:tm, :]


def _prompt_proj(x, g_pre, w_in, conv_w, g_conv, *, a, tm):
    b, t, d = x.shape
    c = conv_w.shape[1]
    nt = t // tm
    rows8 = tm // SUBLANES
    const2 = lambda bi, i: (0, 0)
    tile = lambda width: pl.BlockSpec((1, tm, width), lambda bi, i: (bi, i, 0))
    out_shape = (
        jax.ShapeDtypeStruct((b, t, a), F32),
        jax.ShapeDtypeStruct((b, t, a), F32),
        jax.ShapeDtypeStruct((b, t, a), BF16),
        jax.ShapeDtypeStruct((b, t, a), BF16),
        jax.ShapeDtypeStruct((b, t, a), BF16),
        jax.ShapeDtypeStruct((b, t, c), BF16),
        jax.ShapeDtypeStruct((b, nt, SUBLANES, c), F32),
    )
    return pl.pallas_call(
        functools.partial(_proj_kernel, q_scale=HEAD_DIM ** -0.5 * LOG2E),
        grid=(b, nt),
        in_specs=[
            tile(d),
            pl.BlockSpec((1, SUBLANES, d), lambda bi, i: (bi, jnp.maximum(i * rows8 - 1, 0), 0)),
            pl.BlockSpec(g_pre.shape, const2),
            pl.BlockSpec(w_in.shape, const2),
            pl.BlockSpec(conv_w.shape, const2),
            pl.BlockSpec(g_conv.shape, const2),
        ],
        out_specs=(tile(a), tile(a), tile(a), tile(a), tile(a), tile(c),
                   pl.BlockSpec((1, 1, SUBLANES, c), lambda bi, i: (bi, i, 0, 0))),
        out_shape=out_shape,
        scratch_shapes=[pltpu.VMEM((tm + SUBLANES, c), F32)],
        compiler_params=pltpu.CompilerParams(
            dimension_semantics=("parallel", "arbitrary"), vmem_limit_bytes=VMEM_LIMIT_BYTES),
        name="prompt_proj",
    )(x, x, g_pre, w_in, conv_w, g_conv)


def _attn_kernel(bias_ref, q_ref, k_ref, v_ref, o_ref, q_scr, tri_scr, carry_scr, acc_scr, *, tk, span_widths):
    hp = pl.program_id(1)
    i = pl.program_id(2)
    tq, pair = q_ref.shape[1], q_ref.shape[2]
    n_heads_blk = pair // HEAD_DIM

    q = q_ref[0]
    lane = lax.broadcasted_iota(jnp.int32, (tq, pair), 1)
    for hh in range(n_heads_blk):
        in_head = (lane >= hh * HEAD_DIM) & (lane < (hh + 1) * HEAD_DIM)
        q_scr[hh] = jnp.where(in_head, q, jnp.zeros_like(q))

    @pl.when(i == 0)
    def _():
        r = lax.broadcasted_iota(jnp.int32, (tk, tk), 0)
        cidx = lax.broadcasted_iota(jnp.int32, (tk, tk), 1)
        tri_scr[...] = jnp.where(r >= cidx, 1.0, 0.0).astype(BF16)

    acc_scr[...] = jnp.zeros_like(acc_scr)
    carry_scr[...] = jnp.zeros_like(carry_scr)
    biases = [bias_ref[n_heads_blk * hp + hh] * LOG2E for hh in range(n_heads_blk)]

    def span(j_lo, n_sub, mask):
        ks = pl.ds(pl.multiple_of(j_lo * tk, tk), n_sub * tk)
        kb = k_ref[0, ks, :]
        vb = v_ref[0, ks, :]
        heads = range(n_heads_blk)
        carries = [carry_scr[hh] for hh in heads]
        accs = [acc_scr[hh] for hh in heads]
        zs, pbs = {}, {}

        def logits(s):
            rows = slice(s * tk, (s + 1) * tk)
            for hh in heads:
                zs[s, hh] = _dot_nt(q_scr[hh], kb[rows, :]) + biases[hh]

        def neg_log(s):
            for hh in heads:
                p = _neg_log2_one_minus_beta(zs[s, hh])
                if mask is not None:
                    p = jnp.where(mask, p, 0.0)
                pbs[s, hh] = p.astype(BF16)

        def weigh(s):
            rows = slice(s * tk, (s + 1) * tk)
            for hh in heads:
                incl = _dot(pbs.pop((s, hh)), tri_scr[...])
                carry = jnp.concatenate([carries[hh]] * (tk // LANES), axis=1)
                w = jnp.exp2(zs.pop((s, hh)) - incl - carry)
                if mask is not None:
                    w = jnp.where(mask, w, 0.0)
                accs[hh] = accs[hh] + _dot(w.astype(BF16), vb[rows, :])
                carries[hh] = carries[hh] + incl[:, 0:1]

        order = list(reversed(range(n_sub)))
        for step in range(n_sub + 2):
            if step < n_sub:
                logits(order[step])
            if 1 <= step <= n_sub:
                neg_log(order[step - 1])
            if step >= 2:
                weigh(order[step - 2])
        for hh in heads:
            acc_scr[hh] = accs[hh]
            carry_scr[hh] = carries[hh]

    qr = lax.broadcasted_iota(jnp.int32, (tq, tk), 0)
    kc = lax.broadcasted_iota(jnp.int32, (tq, tk), 1)
    span(i, 1, kc < qr)

    left = i
    for width in span_widths:
        n_steps = left // width

        def body(s, _, hi=left, width=width):
            span(hi - (s + 1) * width, width, None)
            return 0
        lax.fori_loop(0, n_steps, body, 0)
        left = left - n_steps * width

    out = acc_scr[0]
    for hh in range(1, n_heads_blk):
        out = jnp.where(lane >= hh * HEAD_DIM, acc_scr[hh], out)
    o_ref[0] = out


def _prompt_attn(bias, qb, kb, vb, *, tq):
    b, t, a = qb.shape
    tk = tq
    pair = LANES
    return pl.pallas_call(
        functools.partial(_attn_kernel, tk=tk, span_widths=ATTN_SPAN_WIDTHS),
        grid=(b, a // pair, t // tq),
        in_specs=[
            pl.BlockSpec(memory_space=pltpu.SMEM),
            pl.BlockSpec((1, tq, pair), lambda bi, hp, i: (bi, i, hp)),
            pl.BlockSpec((1, t, pair), lambda bi, hp, i: (bi, 0, hp)),
            pl.BlockSpec((1, t, pair), lambda bi, hp, i: (bi, 0, hp)),
        ],
        out_specs=pl.BlockSpec((1, tq, pair), lambda bi, hp, i: (bi, i, hp)),
        out_shape=jax.ShapeDtypeStruct((b, t, a), F32),
        scratch_shapes=[
            pltpu.VMEM((pair // HEAD_DIM, tq, pair), BF16),
            pltpu.VMEM((tk, tk), BF16),
            pltpu.VMEM((pair // HEAD_DIM, tq, LANES), F32),
            pltpu.VMEM((pair // HEAD_DIM, tq, pair), F32),
        ],
        compiler_params=pltpu.CompilerParams(
            dimension_semantics=("parallel", "parallel", "arbitrary"),
            vmem_limit_bytes=VMEM_LIMIT_BYTES),
        name="prompt_attn",
    )(bias, qb, kb, vb)


def _mix_mlp_kernel(x_ref, a_ref, sn_ref, ga_ref, gpm_ref, gpre_ref, gpost_ref,
                    woa_ref, wos_ref, wup_ref, wdn_ref, y_ref, *, ff_chunk):
    an = _rms(a_ref[...], ga_ref[...]).astype(BF16)
    mix = _dot(an, woa_ref[...]) + _dot(sn_ref[...], wos_ref[...])
    x1 = x_ref[...] + _rms(mix, gpm_ref[...])
    hm = _rms(x1, gpre_ref[...]).astype(BF16)
    m = jnp.zeros_like(x1)
    for j in range(wup_ref.shape[1] // ff_chunk):
        up = jnp.maximum(_dot(hm, wup_ref[:, j * ff_chunk:(j + 1) * ff_chunk]), 0.0)
        m = m + _dot((up * up).astype(BF16), wdn_ref[j * ff_chunk:(j + 1) * ff_chunk, :])
    y_ref[...] = x1 + _rms(m, gpost_ref[...])


def _mix_mlp(x, a, sn, g_attn, g_post_mix, g_pre_mlp, g_post_mlp, wo_a, wo_s, w_up, w_dn, *, tm):
    n, d = x.shape
    const = lambda i: (0, 0)
    resident = lambda arr: pl.BlockSpec(arr.shape, const, pipeline_mode=pl.Buffered(1))
    return pl.pallas_call(
        functools.partial(_mix_mlp_kernel, ff_chunk=min(1024, w_up.shape[1])),
        grid=(n // tm,),
        in_specs=[
            pl.BlockSpec((tm, d), lambda i: (i, 0)),
            pl.BlockSpec((tm, a.shape[1]), lambda i: (i, 0)),
            pl.BlockSpec((tm, sn.shape[1]), lambda i: (i, 0)),
            resident(g_attn), resident(g_post_mix), resident(g_pre_mlp), resident(g_post_mlp),
            resident(wo_a), resident(wo_s), resident(w_up), resident(w_dn),
        ],
        out_specs=pl.BlockSpec((tm, d), lambda i: (i, 0)),
        out_shape=jax.ShapeDtypeStruct((n, d), F32),
        compiler_params=pltpu.CompilerParams(
            dimension_semantics=("parallel",), vmem_limit_bytes=VMEM_LIMIT_BYTES),
        name="mix_mlp",
    )(x, a, sn, g_attn, g_post_mix, g_pre_mlp, g_post_mlp, wo_a, wo_s, w_up, w_dn)


def _sample_proj_kernel(x_ref, st0_ref, st1_ref, g_ref, w_ref, cw_ref, gconv_ref,
                        q_ref, k_ref, v_ref, cu_ref, sn_ref, *, scale):
    a = k_ref.shape[1]
    c = sn_ref.shape[1]
    h = _rms(x_ref[...], g_ref[...]).astype(BF16)
    q_ref[...] = _dot(h, w_ref[:, 0:a]) * scale
    k_ref[...] = _dot(h, w_ref[:, a:2 * a])
    v_ref[...] = _dot(h, w_ref[:, 2 * a:3 * a])
    gb = _dot(h, w_ref[:, 3 * a:3 * a + c])
    cu = _dot(h, w_ref[:, 3 * a + c:3 * a + 2 * c]) * _dot(h, w_ref[:, 3 * a + 2 * c:3 * a + 3 * c])
    cu_ref[...] = cu
    y = cw_ref[0:1, :] * st0_ref[...] + cw_ref[1:2, :] * st1_ref[...] + cw_ref[2:3, :] * cu
    sn_ref[...] = _rms(gb * y, gconv_ref[...]).astype(BF16)


def _sample_proj(x, st0, st1, g_pre, w_in, conv_w, g_conv, *, a):
    n = x.shape[0]
    c = conv_w.shape[1]
    return pl.pallas_call(
        functools.partial(_sample_proj_kernel, scale=HEAD_DIM ** -0.5 * LOG2E),
        out_shape=(
            jax.ShapeDtypeStruct((n, a), F32), jax.ShapeDtypeStruct((n, a), F32),
            jax.ShapeDtypeStruct((n, a), F32), jax.ShapeDtypeStruct((n, c), F32),
            jax.ShapeDtypeStruct((n, c), BF16),
        ),
        compiler_params=pltpu.CompilerParams(vmem_limit_bytes=VMEM_LIMIT_BYTES),
        name="sample_proj",
    )(x, st0, st1, g_pre, w_in, conv_w, g_conv)


def _paged_attn_kernel(pt_ref, q_ref, knew_ref, vnew_ref, bias_ref, *rest, n_pg):
    k_refs = rest[:n_pg]
    v_refs = rest[n_pg:2 * n_pg]
    o_ref = rest[2 * n_pg]
    qbd_scr, tri_scr, carry_scr, acc_scr = rest[2 * n_pg + 1:]
    del pt_ref
    cstep = pl.program_id(1)
    n_heads, hd, page = k_refs[0].shape[1:]
    a = n_heads * hd
    bias = bias_ref[...]

    def weights(z, carry, mask):
        p = _neg_log2_one_minus_beta(z)
        if mask is not None:
            p = jnp.where(mask, p, 0.0)
        incl = _dot(p.astype(BF16), tri_scr[...])
        w = jnp.exp2(z - incl - carry)
        if mask is not None:
            w = jnp.where(mask, w, 0.0)
        return w, incl[:, 0:1]

    @pl.when(cstep == 0)
    def _():
        sub = lax.broadcasted_iota(jnp.int32, (n_heads, a), 0)
        lane = lax.broadcasted_iota(jnp.int32, (n_heads, a), 1)
        q_all = jnp.concatenate([q_ref[0]] * n_heads, axis=1)
        qbd_scr[...] = jnp.where(lane // hd == sub, q_all, 0.0).astype(BF16)
        r = lax.broadcasted_iota(jnp.int32, (page, page), 0)
        c = lax.broadcasted_iota(jnp.int32, (page, page), 1)
        tri_scr[...] = jnp.where(r >= c, 1.0, 0.0).astype(BF16)
        n_new = knew_ref.shape[2]
        slot = lax.broadcasted_iota(jnp.int32, (n_heads, page), 1)
        q_off = 0
        visible = (slot < q_off) & (slot < n_new)
        k_new = jnp.broadcast_to(knew_ref[0][:, 0:1], (a, page)).astype(BF16)
        v_new = jnp.broadcast_to(vnew_ref[0][:, 0:1], (a, page)).astype(BF16)
        z = _dot(qbd_scr[...], k_new) + bias
        w, tot = weights(z, jnp.zeros((n_heads, page), F32), visible)
        carry_scr[...] = jnp.broadcast_to(tot, (n_heads, page))
        acc_scr[...] = _dot_nt(w.astype(BF16), v_new)

    qbd = qbd_scr[...]
    zs = [_dot(qbd, k_refs[r][0].reshape(a, page).astype(BF16)) + bias for r in range(n_pg)]
    z = jnp.concatenate(zs, axis=0)
    p = _neg_log2_one_minus_beta(z)
    incl = _dot(p.astype(BF16), tri_scr[...])
    carry = carry_scr[...]
    acc = acc_scr[...]
    for r in reversed(range(n_pg)):
        rows = slice(r * n_heads, (r + 1) * n_heads)
        w = jnp.exp2(zs[r] - incl[rows] - carry)
        acc = acc + _dot_nt(w.astype(BF16), v_refs[r][0].reshape(a, page).astype(BF16))
        carry = carry + incl[rows, 0:1]
    carry_scr[...] = carry
    acc_scr[...] = acc

    @pl.when(cstep == pl.num_programs(1) - 1)
    def _():
        sub = lax.broadcasted_iota(jnp.int32, (n_heads, a), 0)
        lane = lax.broadcasted_iota(jnp.int32, (n_heads, a), 1)
        o_ref[0] = jnp.sum(jnp.where(lane // hd == sub, acc, 0.0), axis=0, keepdims=True)


def _paged_attn(page_table, q, k_new, v_new, bias, cache_kt, cache_vt):
    n, n_heads, hd = q.shape
    n_pages = page_table.shape[1]
    page = cache_kt.shape[3]
    a = n_heads * hd
    assert hd == HEAD_DIM
    n_pg = _pick_tile(n_pages, SAMPLE_PAGES_PER_STEP)
    n_steps = n_pages // n_pg
    bias_tile = jnp.broadcast_to(bias[:, None], (n_heads, page))

    def page_map(bi, c, pt, *, r):
        return (pt[bi * n_pages + (n_steps - 1 - c) * n_pg + r], 0, 0, 0)

    page_specs = [pl.BlockSpec((1, n_heads, hd, page), functools.partial(page_map, r=r))
                  for r in range(n_pg)]
    new_spec = pl.BlockSpec((1,) + k_new.shape[1:], lambda bi, c, pt: (bi, 0, 0))
    grid_spec = pltpu.PrefetchScalarGridSpec(
        num_scalar_prefetch=1,
        grid=(n, n_steps),
        in_specs=[pl.BlockSpec((1, n_heads, hd), lambda bi, c, pt: (bi, 0, 0)), new_spec, new_spec,
                  pl.BlockSpec((n_heads, page), lambda bi, c, pt: (0, 0))] + page_specs + page_specs,
        out_specs=pl.BlockSpec((1, 1, a), lambda bi, c, pt: (bi, 0, 0)),
        scratch_shapes=[
            pltpu.VMEM((n_heads, a), BF16),
            pltpu.VMEM((page, page), BF16),
            pltpu.VMEM((n_heads, page), F32),
            pltpu.VMEM((n_heads, a), F32),
        ],
    )
    out = pl.pallas_call(
        functools.partial(_paged_attn_kernel, n_pg=n_pg),
        grid_spec=grid_spec,
        out_shape=jax.ShapeDtypeStruct((n, 1, a), F32),
        compiler_params=pltpu.CompilerParams(
            dimension_semantics=("parallel", "arbitrary"), vmem_limit_bytes=VMEM_LIMIT_BYTES),
        name="sample_attn",
    )(page_table.reshape(-1), q, k_new, v_new, bias_tile,
      *([cache_kt] * n_pg), *([cache_vt] * n_pg))
    return out[:, 0, :]


def _pick_tile(n, want):
    t = min(n, want)
    while n % t:
        t //= 2
    return t


def _layer_step(xp, xs, cache_k, cache_v, state_conv, page_table,
                g_pre_mix, g_post_mix, g_pre_mlp, g_post_mlp,
                w_in, attn_bias, conv_w, g_attn, g_conv, w_out, w_up, w_down):
    b, t, d = xp.shape
    n_heads = attn_bias.shape[0]
    a = n_heads * HEAD_DIM
    c = conv_w.shape[1]
    row = lambda v: v.reshape(1, -1)
    g_pre_mix, g_post_mix, g_pre_mlp, g_post_mlp = map(row, (g_pre_mix, g_post_mix, g_pre_mlp, g_post_mlp))
    g_attn, g_conv = row(g_attn), row(g_conv)

    w_in_b = w_in.astype(BF16)
    wo_a = w_out[:a].astype(BF16)
    wo_s = w_out[a:].astype(BF16)
    w_up_b = w_up.astype(BF16)
    w_dn_b = w_down.astype(BF16)

    k_p, v_p, qb, kb, vb, sn_p, tails = _prompt_proj(
        xp, g_pre_mix, w_in_b, conv_w, g_conv, a=a, tm=_pick_tile(t, 512))
    a_p = _prompt_attn(attn_bias, qb, kb, vb, tq=_pick_tile(t, 256))
    y_p = _mix_mlp(xp.reshape(b * t, d), a_p.reshape(b * t, a), sn_p.reshape(b * t, c),
                   g_attn, g_post_mix, g_pre_mlp, g_post_mlp, wo_a, wo_s, w_up_b, w_dn_b,
                   tm=_pick_tile(b * t, 512))
    new_conv_p = tails[:, -1, SUBLANES - (CONV_TAPS - 1):, :]

    n = xs.shape[0]
    xs2 = xs.reshape(n, d)
    q_s, k_s, v_s, cu_s, sn_s = _sample_proj(
        xs2, state_conv[:, 0, :], state_conv[:, 1, :], g_pre_mix, w_in_b, conv_w, g_conv, a=a)
    to_keys_minor = lambda cache: jnp.transpose(cache, (0, 2, 3, 1))
    a_s = _paged_attn(page_table, q_s.reshape(n, n_heads, HEAD_DIM), k_s[:, :, None], v_s[:, :, None],
                      attn_bias * LOG2E, to_keys_minor(cache_k), to_keys_minor(cache_v))
    y_s = _mix_mlp(xs2, a_s, sn_s, g_attn, g_post_mix, g_pre_mlp, g_post_mlp,
                   wo_a, wo_s, w_up_b, w_dn_b, tm=_pick_tile(n, 512))
    new_conv_s = jnp.stack([state_conv[:, 1, :], cu_s], axis=1)

    return (y_p.reshape(b, t, d), y_s.reshape(n, 1, d),
            k_p.reshape(b, t, n_heads, HEAD_DIM), v_p.reshape(b, t, n_heads, HEAD_DIM), new_conv_p,
            k_s.reshape(n, 1, n_heads, HEAD_DIM), v_s.reshape(n, 1, n_heads, HEAD_DIM), new_conv_s)


def kernel(x_prompt, x_sample, cache_k, cache_v, state_conv, page_table, norm_pre_mix, norm_post_mix,
           norm_pre_mlp, norm_post_mlp, w_in, attn_logit_bias, conv_w, norm_attn_out, norm_conv_out,
           w_out, w_up, w_down):
    depth = w_in.shape[0]
    assert x_sample.shape[1] == 1, "the sample path handles one new token per sequence"
    xp, xs = x_prompt, x_sample
    outs = []
    for l in range(depth):
        res = _layer_step(
            xp, xs, cache_k[l], cache_v[l], state_conv[l], page_table,
            norm_pre_mix[l], norm_post_mix[l], norm_pre_mlp[l], norm_post_mlp[l],
            w_in[l], attn_logit_bias[l], conv_w[l], norm_attn_out[l], norm_conv_out[l],
            w_out[l], w_up[l], w_down[l])
        xp, xs = res[0], res[1]
        outs.append(res[2:])
    stacked = tuple(jnp.stack([o[j] for o in outs]) for j in range(6))
    return (xp, xs) + stacked
```

```python
import functools

import jax
import jax.numpy as jnp
from jax import lax
from jax.experimental import pallas as pl
from jax.experimental.pallas import tpu as pltpu

HEAD_DIM = 64
CONV_TAPS = 3
EPS = 1e-6
SUBLANES = 8
LANES = 128
VMEM_LIMIT_BYTES = 56 * 1024 * 1024
LOG2E = 1.4426950408889634
EXP2_SAFE = 126.0
ATTN_SPAN_WIDTHS = (16, 8, 4, 2, 1)
SAMPLE_PAGES_PER_STEP = 16

F32 = jnp.float32
BF16 = jnp.bfloat16


def _rms(x, g):
    return x * lax.rsqrt(jnp.mean(x * x, axis=-1, keepdims=True) + EPS) * g


def _floor_div(x, d):
    if d & (d - 1) == 0:
        return lax.shift_right_logical(x, jnp.int32(d.bit_length() - 1))
    return x // d


def _dot(a, b):
    return jnp.dot(a, b, preferred_element_type=F32)


def _dot_nt(a, b):
    return lax.dot_general(a, b, (((1,), (1,)), ((), ())), preferred_element_type=F32)


def _neg_log2_one_minus_beta(z):
    return jnp.maximum(z, jnp.log2(1.0 + jnp.exp2(jnp.minimum(z, EXP2_SAFE))))


def _proj_kernel(x_ref, xprev_ref, g_ref, w_ref, cw_ref, gconv_ref,
                 k_ref, v_ref, qb_ref, kb_ref, vb_ref, sn_ref, tail_ref, cu_scr, *, q_scale):
    i = pl.program_id(1)
    tm = x_ref.shape[1]
    a = k_ref.shape[2]
    c = sn_ref.shape[2]
    g = g_ref[...]
    h = _rms(x_ref[0], g).astype(BF16)

    qb_ref[0] = (_dot(h, w_ref[:, 0:a]) * q_scale).astype(BF16)
    k = _dot(h, w_ref[:, a:2 * a])
    k_ref[0] = k
    kb_ref[0] = k.astype(BF16)
    v = _dot(h, w_ref[:, 2 * a:3 * a])
    v_ref[0] = v
    vb_ref[0] = v.astype(BF16)

    o = 3 * a
    gb = _dot(h, w_ref[:, o:o + c])
    cu = _dot(h, w_ref[:, o + c:o + 2 * c]) * _dot(h, w_ref[:, o + 2 * c:o + 3 * c])
    hp = _rms(xprev_ref[0], g).astype(BF16)
    cup = _dot(hp, w_ref[:, o + c:o + 2 * c]) * _dot(hp, w_ref[:, o + 2 * c:o + 3 * c])
    cu_scr[0:SUBLANES, :] = jnp.where(i == 0, 0.0, cup)
    cu_scr[SUBLANES:SUBLANES + tm, :] = cu
    y = (cw_ref[0:1, :] * cu_scr[SUBLANES - 2:SUBLANES - 2 + tm, :]
         + cw_ref[1:2, :] * cu_scr[SUBLANES - 1:SUBLANES - 1 + tm, :]
         + cw_ref[2:3, :] * cu)
    sn_ref[0] = _rms(gb * y, gconv_ref[...]).astype(BF16)
    tail_ref[0, 0] = cu[tm - SUBLANES:tm, :]


def _prompt_proj(x, g_pre, w_in, conv_w, g_conv, *, a, tm):
    b, t, d = x.shape
    c = conv_w.shape[1]
    nt = t // tm
    rows8 = tm // SUBLANES
    const2 = lambda bi, i: (0, 0)
    tile = lambda width: pl.BlockSpec((1, tm, width), lambda bi, i: (bi, i, 0))
    out_shape = (
        jax.ShapeDtypeStruct((b, t, a), F32),
        jax.ShapeDtypeStruct((b, t, a), F32),
        jax.ShapeDtypeStruct((b, t, a), BF16),
        jax.ShapeDtypeStruct((b, t, a), BF16),
        jax.ShapeDtypeStruct((b, t, a), BF16),
        jax.ShapeDtypeStruct((b, t, c), BF16),
        jax.ShapeDtypeStruct((b, nt, SUBLANES, c), F32),
    )
    return pl.pallas_call(
        functools.partial(_proj_kernel, q_scale=HEAD_DIM ** -0.5 * LOG2E),
        grid=(b, nt),
        in_specs=[
            tile(d),
            pl.BlockSpec((1, SUBLANES, d), lambda bi, i: (bi, jnp.maximum(i * rows8 - 1, 0), 0)),
            pl.BlockSpec(g_pre.shape, const2),
            pl.BlockSpec(w_in.shape, const2),
            pl.BlockSpec(conv_w.shape, const2),
            pl.BlockSpec(g_conv.shape, const2),
        ],
        out_specs=(tile(a), tile(a), tile(a), tile(a), tile(a), tile(c),
                   pl.BlockSpec((1, 1, SUBLANES, c), lambda bi, i: (bi, i, 0, 0))),
        out_shape=out_shape,
        scratch_shapes=[pltpu.VMEM((tm + SUBLANES, c), F32)],
        compiler_params=pltpu.CompilerParams(
            dimension_semantics=("parallel", "arbitrary"), vmem_limit_bytes=VMEM_LIMIT_BYTES),
        name="prompt_proj",
    )(x, x, g_pre, w_in, conv_w, g_conv)


def _attn_kernel(bias_ref, q_ref, k_ref, v_ref, o_ref, q_scr, tri_scr, carry_scr, acc_scr, *, tk, span_widths):
    _attn_step(pl.program_id(1), pl.program_id(2), bias_ref, q_ref, k_ref, v_ref, o_ref,
               q_scr, tri_scr, carry_scr, acc_scr, tk=tk, span_widths=span_widths)


def _attn_step(hp, i, bias_ref, q_ref, k_ref, v_ref, o_ref, q_scr, tri_scr, carry_scr, acc_scr,
               *, tk, span_widths, diag_hooks=()):
    tq, pair = q_ref.shape[1], q_ref.shape[2]
    n_heads_blk = pair // HEAD_DIM

    @pl.when(i == 0)
    def _():
        r = lax.broadcasted_iota(jnp.int32, (tk, tk), 0)
        cidx = lax.broadcasted_iota(jnp.int32, (tk, tk), 1)
        tri_scr[...] = jnp.where(r >= cidx, 1.0, 0.0).astype(BF16)

    q = q_ref[0]
    lane = lax.broadcasted_iota(jnp.int32, (tq, pair), 1)
    for hh in range(n_heads_blk):
        in_head = (lane >= hh * HEAD_DIM) & (lane < (hh + 1) * HEAD_DIM)
        q_scr[hh] = jnp.where(in_head, q, jnp.zeros_like(q))
    acc_scr[...] = jnp.zeros_like(acc_scr)
    carry_scr[...] = jnp.zeros_like(carry_scr)
    biases = [bias_ref[n_heads_blk * hp + hh] * LOG2E for hh in range(n_heads_blk)]

    def span(j_lo, n_sub, mask, hooks=()):
        ks = pl.ds(pl.multiple_of(j_lo * tk, tk), n_sub * tk)
        kb = k_ref[0, ks, :]
        vb = v_ref[0, ks, :]
        heads = range(n_heads_blk)
        carries = [carry_scr[hh] for hh in heads]
        accs = [acc_scr[hh] for hh in heads]
        zs, pbs = {}, {}

        def logits(s):
            rows = slice(s * tk, (s + 1) * tk)
            for hh in heads:
                zs[s, hh] = _dot_nt(q_scr[hh], kb[rows, :]) + biases[hh]

        def neg_log(s):
            for hh in heads:
                p = _neg_log2_one_minus_beta(zs[s, hh])
                if mask is not None:
                    p = jnp.where(mask, p, 0.0)
                pbs[s, hh] = p.astype(BF16)

        def weigh(s):
            rows = slice(s * tk, (s + 1) * tk)
            for hh in heads:
                incl = _dot(pbs.pop((s, hh)), tri_scr[...])
                carry = jnp.concatenate([carries[hh]] * (tk // LANES), axis=1)
                w = jnp.exp2(zs.pop((s, hh)) - incl - carry)
                if mask is not None:
                    w = jnp.where(mask, w, 0.0)
                accs[hh] = accs[hh] + _dot(w.astype(BF16), vb[rows, :])
                carries[hh] = carries[hh] + incl[:, 0:1]

        order = list(reversed(range(n_sub)))
        if hooks:
            hooks[0]()
        for step in range(n_sub + 2):
            if step < n_sub:
                logits(order[step])
            if 1 <= step <= n_sub:
                neg_log(order[step - 1])
            if step >= 2:
                weigh(order[step - 2])
            if step + 1 < len(hooks):
                hooks[step + 1]()
        for hh in heads:
            acc_scr[hh] = accs[hh]
            carry_scr[hh] = carries[hh]

    qr = lax.broadcasted_iota(jnp.int32, (tq, tk), 0)
    kc = lax.broadcasted_iota(jnp.int32, (tq, tk), 1)
    span(i, 1, kc < qr, diag_hooks)

    left = i
    for width in span_widths:
        n_steps = left // width

        def body(s, _, hi=left, width=width):
            span(hi - (s + 1) * width, width, None)
            return 0
        lax.fori_loop(0, n_steps, body, 0)
        left = left - n_steps * width

    out = acc_scr[0]
    for hh in range(1, n_heads_blk):
        out = jnp.where(lane >= hh * HEAD_DIM, acc_scr[hh], out)
    o_ref[0] = out


def _prompt_attn(bias, qb, kb, vb, *, tq):
    b, t, a = qb.shape
    tk = tq
    pair = LANES
    return pl.pallas_call(
        functools.partial(_attn_kernel, tk=tk, span_widths=ATTN_SPAN_WIDTHS),
        grid=(b, a // pair, t // tq),
        in_specs=[
            pl.BlockSpec(memory_space=pltpu.SMEM),
            pl.BlockSpec((1, tq, pair), lambda bi, hp, i: (bi, i, hp)),
            pl.BlockSpec((1, t, pair), lambda bi, hp, i: (bi, 0, hp)),
            pl.BlockSpec((1, t, pair), lambda bi, hp, i: (bi, 0, hp)),
        ],
        out_specs=pl.BlockSpec((1, tq, pair), lambda bi, hp, i: (bi, i, hp)),
        out_shape=jax.ShapeDtypeStruct((b, t, a), F32),
        scratch_shapes=[
            pltpu.VMEM((pair // HEAD_DIM, tq, pair), BF16),
            pltpu.VMEM((tk, tk), BF16),
            pltpu.VMEM((pair // HEAD_DIM, tq, LANES), F32),
            pltpu.VMEM((pair // HEAD_DIM, tq, pair), F32),
        ],
        compiler_params=pltpu.CompilerParams(
            dimension_semantics=("parallel", "parallel", "arbitrary"),
            vmem_limit_bytes=VMEM_LIMIT_BYTES),
        name="prompt_attn",
    )(bias, qb, kb, vb)


def _mix_mlp_kernel(x_ref, a_ref, sn_ref, ga_ref, gpm_ref, gpre_ref, gpost_ref,
                    woa_ref, wos_ref, wup_ref, wdn_ref, y_ref, *, ff_chunk):
    an = _rms(a_ref[...], ga_ref[...]).astype(BF16)
    mix = _dot(an, woa_ref[...]) + _dot(sn_ref[...], wos_ref[...])
    x1 = x_ref[...] + _rms(mix, gpm_ref[...])
    hm = _rms(x1, gpre_ref[...]).astype(BF16)
    m = jnp.zeros_like(x1)
    for j in range(wup_ref.shape[1] // ff_chunk):
        up = jnp.maximum(_dot(hm, wup_ref[:, j * ff_chunk:(j + 1) * ff_chunk]), 0.0)
        m = m + _dot((up * up).astype(BF16), wdn_ref[j * ff_chunk:(j + 1) * ff_chunk, :])
    y_ref[...] = x1 + _rms(m, gpost_ref[...])


def _mix_mlp(x, a, sn, g_attn, g_post_mix, g_pre_mlp, g_post_mlp, wo_a, wo_s, w_up, w_dn, *, tm):
    n, d = x.shape
    const = lambda i: (0, 0)
    resident = lambda arr: pl.BlockSpec(arr.shape, const, pipeline_mode=pl.Buffered(1))
    return pl.pallas_call(
        functools.partial(_mix_mlp_kernel, ff_chunk=min(1024, w_up.shape[1])),
        grid=(n // tm,),
        in_specs=[
            pl.BlockSpec((tm, d), lambda i: (i, 0)),
            pl.BlockSpec((tm, a.shape[1]), lambda i: (i, 0)),
            pl.BlockSpec((tm, sn.shape[1]), lambda i: (i, 0)),
            resident(g_attn), resident(g_post_mix), resident(g_pre_mlp), resident(g_post_mlp),
            resident(wo_a), resident(wo_s), resident(w_up), resident(w_dn),
        ],
        out_specs=pl.BlockSpec((tm, d), lambda i: (i, 0)),
        out_shape=jax.ShapeDtypeStruct((n, d), F32),
        compiler_params=pltpu.CompilerParams(
            dimension_semantics=("parallel",), vmem_limit_bytes=VMEM_LIMIT_BYTES),
        name="mix_mlp",
    )(x, a, sn, g_attn, g_post_mix, g_pre_mlp, g_post_mlp, wo_a, wo_s, w_up, w_dn)


def _sample_proj_kernel(x_ref, st0_ref, st1_ref, g_ref, w_ref, cw_ref, gconv_ref,
                        q_ref, k_ref, v_ref, cu_ref, sn_ref, *, scale):
    a = k_ref.shape[1]
    c = sn_ref.shape[1]
    h = _rms(x_ref[...], g_ref[...]).astype(BF16)
    q_ref[...] = _dot(h, w_ref[:, 0:a]) * scale
    k_ref[...] = _dot(h, w_ref[:, a:2 * a])
    v_ref[...] = _dot(h, w_ref[:, 2 * a:3 * a])
    gb = _dot(h, w_ref[:, 3 * a:3 * a + c])
    cu = _dot(h, w_ref[:, 3 * a + c:3 * a + 2 * c]) * _dot(h, w_ref[:, 3 * a + 2 * c:3 * a + 3 * c])
    cu_ref[...] = cu
    y = cw_ref[0:1, :] * st0_ref[...] + cw_ref[1:2, :] * st1_ref[...] + cw_ref[2:3, :] * cu
    sn_ref[...] = _rms(gb * y, gconv_ref[...]).astype(BF16)


def _sample_proj(x, st0, st1, g_pre, w_in, conv_w, g_conv, *, a):
    n = x.shape[0]
    c = conv_w.shape[1]
    return pl.pallas_call(
        functools.partial(_sample_proj_kernel, scale=HEAD_DIM ** -0.5 * LOG2E),
        out_shape=(
            jax.ShapeDtypeStruct((n, a), F32), jax.ShapeDtypeStruct((n, a), F32),
            jax.ShapeDtypeStruct((n, a), F32), jax.ShapeDtypeStruct((n, c), F32),
            jax.ShapeDtypeStruct((n, c), BF16),
        ),
        compiler_params=pltpu.CompilerParams(vmem_limit_bytes=VMEM_LIMIT_BYTES),
        name="sample_proj",
    )(x, st0, st1, g_pre, w_in, conv_w, g_conv)


def _paged_attn_kernel(pt_ref, q_ref, knew_ref, vnew_ref, bias_ref, *rest, n_pg):
    del pt_ref
    k_refs = rest[:n_pg]
    v_refs = rest[n_pg:2 * n_pg]
    o_ref = rest[2 * n_pg]
    scratch = rest[2 * n_pg + 1:]
    first, logits, weigh, values, last = _paged_attn_stages(
        pl.program_id(1), pl.num_programs(1), q_ref, knew_ref, vnew_ref, bias_ref, k_refs, v_refs, o_ref, *scratch)
    first()
    logits()
    weigh()
    values()
    last()


def _paged_attn_stages(cstep, n_steps, q_ref, knew_ref, vnew_ref, bias_ref, k_refs, v_refs, o_ref,
                       qbd_scr, tri_scr, carry_scr, acc_scr):
    n_pg = len(k_refs)
    n_heads, hd, page = k_refs[0].shape[1:]
    a = n_heads * hd
    bias = bias_ref[...]

    def weights(z, carry, mask):
        p = _neg_log2_one_minus_beta(z)
        if mask is not None:
            p = jnp.where(mask, p, 0.0)
        incl = _dot(p.astype(BF16), tri_scr[...])
        w = jnp.exp2(z - incl - carry)
        if mask is not None:
            w = jnp.where(mask, w, 0.0)
        return w, incl[:, 0:1]

    def first():
        pl.when(cstep == 0)(start_sequence)

    def start_sequence():
        sub = lax.broadcasted_iota(jnp.int32, (n_heads, a), 0)
        lane = lax.broadcasted_iota(jnp.int32, (n_heads, a), 1)
        q_all = jnp.concatenate([q_ref[0]] * n_heads, axis=1)
        qbd_scr[...] = jnp.where(lane // hd == sub, q_all, 0.0).astype(BF16)
        r = lax.broadcasted_iota(jnp.int32, (page, page), 0)
        c = lax.broadcasted_iota(jnp.int32, (page, page), 1)
        tri_scr[...] = jnp.where(r >= c, 1.0, 0.0).astype(BF16)
        n_new = knew_ref.shape[2]
        slot = lax.broadcasted_iota(jnp.int32, (n_heads, page), 1)
        q_off = 0
        visible = (slot < q_off) & (slot < n_new)
        k_new = jnp.broadcast_to(knew_ref[0][:, 0:1], (a, page)).astype(BF16)
        v_new = jnp.broadcast_to(vnew_ref[0][:, 0:1], (a, page))
        z = _dot(qbd_scr[...], k_new) + bias
        w, tot = weights(z, jnp.zeros((n_heads, page), F32), visible)
        carry_scr[...] = jnp.broadcast_to(tot, (n_heads, page))
        w_rows = jnp.concatenate([jnp.broadcast_to(w[h:h + 1, :], (hd, page)) for h in range(n_heads)], axis=0)
        acc_scr[...] = w_rows * v_new

    state = {}

    def logits():
        qbd = qbd_scr[...]
        state["zs"] = [_dot(qbd, k_refs[r][0].reshape(a, page).astype(BF16)) + bias for r in range(n_pg)]

    def weigh():
        z = jnp.concatenate(state["zs"], axis=0)
        p = _neg_log2_one_minus_beta(z)
        incl = _dot(p.astype(BF16), tri_scr[...])
        carry = carry_scr[...]
        ws = [None] * n_pg
        for r in reversed(range(n_pg)):
            rows = slice(r * n_heads, (r + 1) * n_heads)
            ws[r] = jnp.exp2(state["zs"][r] - incl[rows] - carry)
            carry = carry + incl[rows, 0:1]
        carry_scr[...] = carry
        state["ws"] = ws

    def values():
        for h in range(n_heads):
            rows = slice(h * hd, (h + 1) * hd)
            acc = acc_scr[rows, :]
            for r in range(n_pg):
                acc = acc + jnp.broadcast_to(state["ws"][r][h:h + 1, :], (hd, page)) * v_refs[r][0, h]
            acc_scr[rows, :] = acc

    def last():
        pl.when(cstep == n_steps - 1)(end_sequence)

    def end_sequence():
        o_ref[0] = jnp.sum(acc_scr[...], axis=1, keepdims=True)

    return first, logits, weigh, values, last


def _paged_attn(page_table, q, k_new, v_new, bias, cache_kt, cache_vt, *, n_pg):
    n, n_heads, hd = q.shape
    n_pages = page_table.shape[1]
    page = cache_kt.shape[3]
    a = n_heads * hd
    assert hd == HEAD_DIM
    n_steps = n_pages // n_pg
    bias_tile = jnp.broadcast_to(bias[:, None], (n_heads, page))

    def page_map(bi, c, pt, *, r):
        return (pt[bi * n_pages + (n_steps - 1 - c) * n_pg + r], 0, 0, 0)

    page_specs = [pl.BlockSpec((1, n_heads, hd, page), functools.partial(page_map, r=r))
                  for r in range(n_pg)]
    new_spec = pl.BlockSpec((1,) + k_new.shape[1:], lambda bi, c, pt: (bi, 0, 0))
    grid_spec = pltpu.PrefetchScalarGridSpec(
        num_scalar_prefetch=1,
        grid=(n, n_steps),
        in_specs=[pl.BlockSpec((1, n_heads, hd), lambda bi, c, pt: (bi, 0, 0)), new_spec, new_spec,
                  pl.BlockSpec((n_heads, page), lambda bi, c, pt: (0, 0))] + page_specs + page_specs,
        out_specs=pl.BlockSpec((1, a, 1), lambda bi, c, pt: (bi, 0, 0)),
        scratch_shapes=[
            pltpu.VMEM((n_heads, a), BF16),
            pltpu.VMEM((page, page), BF16),
            pltpu.VMEM((n_heads, page), F32),
            pltpu.VMEM((a, page), F32),
        ],
    )
    out = pl.pallas_call(
        functools.partial(_paged_attn_kernel, n_pg=n_pg),
        grid_spec=grid_spec,
        out_shape=jax.ShapeDtypeStruct((n, a, 1), F32),
        compiler_params=pltpu.CompilerParams(
            dimension_semantics=("parallel", "arbitrary"), vmem_limit_bytes=VMEM_LIMIT_BYTES),
        name="sample_attn",
    )(page_table.reshape(-1), q, k_new, v_new, bias_tile,
      *([cache_kt] * n_pg), *([cache_vt] * n_pg))
    return out[:, :, 0]


def _both_attn_kernel(pt_ref, bias_ref, q_ref, k_ref, v_ref, sq_ref, knew_ref, vnew_ref, sbias_ref, *rest,
                      n_pg, sample_steps, tk, span_widths):
    del pt_ref
    k_refs = rest[:n_pg]
    v_refs = rest[n_pg:2 * n_pg]
    o_ref, so_ref = rest[2 * n_pg:2 * n_pg + 2]
    q_scr, tri_scr, carry_scr, acc_scr, sqbd_scr, stri_scr, scarry_scr, sacc_scr = rest[2 * n_pg + 2:]
    hp, i = pl.program_id(1), pl.program_id(2)
    step = (pl.program_id(0) * pl.num_programs(1) + hp) * pl.num_programs(2) + i
    first, logits, weigh, values, last = _paged_attn_stages(
        step - _floor_div(step, sample_steps) * sample_steps, sample_steps, sq_ref, knew_ref, vnew_ref, sbias_ref, k_refs, v_refs, so_ref,
        sqbd_scr, stri_scr, scarry_scr, sacc_scr)
    first()
    _attn_step(hp, i, bias_ref, q_ref, k_ref, v_ref, o_ref, q_scr, tri_scr, carry_scr, acc_scr,
               tk=tk, span_widths=span_widths, diag_hooks=(logits, weigh, values))
    last()


def _both_attn(bias, qb, kb, vb, page_table, sq, k_new, v_new, cache_kt, cache_vt, *, tq, n_pg):
    b, t, a = qb.shape
    tk = tq
    pair = LANES
    n, n_heads, hd = sq.shape
    n_pages = page_table.shape[1]
    page = cache_kt.shape[3]
    n_hp, n_q = a // pair, t // tq
    sample_steps = n_pages // n_pg
    assert n * sample_steps == b * n_hp * n_q and hd == HEAD_DIM
    sbias = jnp.broadcast_to((bias * LOG2E)[:, None], (n_heads, page))

    def seq(bi, hp, i):
        return _floor_div((bi * n_hp + hp) * n_q + i, sample_steps)

    def page_map(bi, hp, i, pt, *, r):
        step = (bi * n_hp + hp) * n_q + i
        s = _floor_div(step, sample_steps)
        first_page = (sample_steps - 1 - (step - s * sample_steps)) * n_pg
        return (pt[s * n_pages + first_page + r], 0, 0, 0)

    page_specs = [pl.BlockSpec((1, n_heads, hd, page), functools.partial(page_map, r=r)) for r in range(n_pg)]
    new_spec = pl.BlockSpec((1,) + k_new.shape[1:], lambda bi, hp, i, pt: (seq(bi, hp, i), 0, 0))
    grid_spec = pltpu.PrefetchScalarGridSpec(
        num_scalar_prefetch=1,
        grid=(b, n_hp, n_q),
        in_specs=[
            pl.BlockSpec(memory_space=pltpu.SMEM),
            pl.BlockSpec((1, tq, pair), lambda bi, hp, i, pt: (bi, i, hp)),
            pl.BlockSpec((1, t, pair), lambda bi, hp, i, pt: (bi, 0, hp)),
            pl.BlockSpec((1, t, pair), lambda bi, hp, i, pt: (bi, 0, hp)),
            pl.BlockSpec((1, n_heads, hd), lambda bi, hp, i, pt: (seq(bi, hp, i), 0, 0)),
            new_spec, new_spec,
            pl.BlockSpec((n_heads, page), lambda bi, hp, i, pt: (0, 0)),
        ] + page_specs + page_specs,
        out_specs=(
            pl.BlockSpec((1, tq, pair), lambda bi, hp, i, pt: (bi, i, hp)),
            pl.BlockSpec((1, n_heads * hd, 1), lambda bi, hp, i, pt: (seq(bi, hp, i), 0, 0)),
        ),
        scratch_shapes=[
            pltpu.VMEM((pair // HEAD_DIM, tq, pair), BF16),
            pltpu.VMEM((tk, tk), BF16),
            pltpu.VMEM((pair // HEAD_DIM, tq, LANES), F32),
            pltpu.VMEM((pair // HEAD_DIM, tq, pair), F32),
            pltpu.VMEM((n_heads, n_heads * hd), BF16),
            pltpu.VMEM((page, page), BF16),
            pltpu.VMEM((n_heads, page), F32),
            pltpu.VMEM((n_heads * hd, page), F32),
        ],
    )
    a_p, a_s = pl.pallas_call(
        functools.partial(_both_attn_kernel, n_pg=n_pg, sample_steps=sample_steps, tk=tk,
                          span_widths=ATTN_SPAN_WIDTHS),
        grid_spec=grid_spec,
        out_shape=(jax.ShapeDtypeStruct((b, t, a), F32), jax.ShapeDtypeStruct((n, n_heads * hd, 1), F32)),
        compiler_params=pltpu.CompilerParams(
            dimension_semantics=("arbitrary", "arbitrary", "arbitrary"), vmem_limit_bytes=VMEM_LIMIT_BYTES),
        name="both_attn",
    )(page_table.reshape(-1), bias, qb, kb, vb, sq, k_new, v_new, sbias,
      *([cache_kt] * n_pg), *([cache_vt] * n_pg))
    return a_p, a_s[:, :, 0]


def _pick_tile(n, want):
    t = min(n, want)
    while n % t:
        t //= 2
    return t


def _layer_step(xp, xs, cache_k, cache_v, state_conv, page_table,
                g_pre_mix, g_post_mix, g_pre_mlp, g_post_mlp,
                w_in, attn_bias, conv_w, g_attn, g_conv, w_out, w_up, w_down):
    b, t, d = xp.shape
    n_heads = attn_bias.shape[0]
    a = n_heads * HEAD_DIM
    c = conv_w.shape[1]
    row = lambda v: v.reshape(1, -1)
    g_pre_mix, g_post_mix, g_pre_mlp, g_post_mlp = map(row, (g_pre_mix, g_post_mix, g_pre_mlp, g_post_mlp))
    g_attn, g_conv = row(g_attn), row(g_conv)

    w_in_b = w_in.astype(BF16)
    wo_a = w_out[:a].astype(BF16)
    wo_s = w_out[a:].astype(BF16)
    w_up_b = w_up.astype(BF16)
    w_dn_b = w_down.astype(BF16)

    k_p, v_p, qb, kb, vb, sn_p, tails = _prompt_proj(
        xp, g_pre_mix, w_in_b, conv_w, g_conv, a=a, tm=_pick_tile(t, 512))
    n = xs.shape[0]
    xs2 = xs.reshape(n, d)
    q_s, k_s, v_s, cu_s, sn_s = _sample_proj(
        xs2, state_conv[:, 0, :], state_conv[:, 1, :], g_pre_mix, w_in_b, conv_w, g_conv, a=a)

    tq = _pick_tile(t, 256)
    n_pages = page_table.shape[1]
    n_pg = _pick_tile(n_pages, SAMPLE_PAGES_PER_STEP)
    to_keys_minor = lambda cache: jnp.transpose(cache, (0, 2, 3, 1))
    sample_args = (q_s.reshape(n, n_heads, HEAD_DIM), k_s[:, :, None], v_s[:, :, None])
    caches = (to_keys_minor(cache_k), to_keys_minor(cache_v))
    if n * (n_pages // n_pg) == b * (a // LANES) * (t // tq):
        a_p, a_s = _both_attn(attn_bias, qb, kb, vb, page_table, *sample_args, *caches, tq=tq, n_pg=n_pg)
    else:
        a_p = _prompt_attn(attn_bias, qb, kb, vb, tq=tq)
        a_s = _paged_attn(page_table, *sample_args, attn_bias * LOG2E, *caches, n_pg=n_pg)

    y_p = _mix_mlp(xp.reshape(b * t, d), a_p.reshape(b * t, a), sn_p.reshape(b * t, c),
                   g_attn, g_post_mix, g_pre_mlp, g_post_mlp, wo_a, wo_s, w_up_b, w_dn_b,
                   tm=_pick_tile(b * t, 512))
    y_s = _mix_mlp(xs2, a_s, sn_s, g_attn, g_post_mix, g_pre_mlp, g_post_mlp,
                   wo_a, wo_s, w_up_b, w_dn_b, tm=_pick_tile(n, 512))
    new_conv_p = tails[:, -1, SUBLANES - (CONV_TAPS - 1):, :]
    new_conv_s = jnp.stack([state_conv[:, 1, :], cu_s], axis=1)

    return (y_p.reshape(b, t, d), y_s.reshape(n, 1, d),
            k_p.reshape(b, t, n_heads, HEAD_DIM), v_p.reshape(b, t, n_heads, HEAD_DIM), new_conv_p,
            k_s.reshape(n, 1, n_heads, HEAD_DIM), v_s.reshape(n, 1, n_heads, HEAD_DIM), new_conv_s)


def kernel(x_prompt, x_sample, cache_k, cache_v, state_conv, page_table, norm_pre_mix, norm_post_mix,
           norm_pre_mlp, norm_post_mlp, w_in, attn_logit_bias, conv_w, norm_attn_out, norm_conv_out,
           w_out, w_up, w_down):
    depth = w_in.shape[0]
    assert x_sample.shape[1] == 1, "the sample path handles one new token per sequence"
    xp, xs = x_prompt, x_sample
    outs = []
    for l in range(depth):
        res = _layer_step(
            xp, xs, cache_k[l], cache_v[l], state_conv[l], page_table,
            norm_pre_mix[l], norm_post_mix[l], norm_pre_mlp[l], norm_post_mlp[l],
            w_in[l], attn_logit_bias[l], conv_w[l], norm_attn_out[l], norm_conv_out[l],
            w_out[l], w_up[l], w_down[l])
        xp, xs = res[0], res[1]
        outs.append(res[2:])
    stacked = tuple(jnp.stack([o[j] for o in outs]) for j in range(6))
    return (xp, xs) + stacked
```

```python
import functools

import jax
import jax.numpy as jnp
from jax import lax
from jax.experimental import pallas as pl
from jax.experimental.pallas import tpu as pltpu

HEAD_DIM = 64
CONV_TAPS = 3
EPS = 1e-6
SUBLANES = 8
LANES = 128
VMEM_LIMIT_BYTES = 56 * 1024 * 1024
LOG2E = 1.4426950408889634
EXP2_SAFE = 126.0
ATTN_SPAN_WIDTHS = (16, 8, 4, 2, 1)
SAMPLE_PAGES_PER_STEP = 16

F32 = jnp.float32
BF16 = jnp.bfloat16


def _rms(x, g):
    return x * lax.rsqrt(jnp.mean(x * x, axis=-1, keepdims=True) + EPS) * g


def _floor_div(x, d):
    if d & (d - 1) == 0:
        return lax.shift_right_logical(x, jnp.int32(d.bit_length() - 1))
    return x // d


def _dot(a, b):
    return jnp.dot(a, b, preferred_element_type=F32)


def _dot_nt(a, b):
    return lax.dot_general(a, b, (((1,), (1,)), ((), ())), preferred_element_type=F32)


def _neg_log2_one_minus_beta(z):
    return jnp.maximum(z, jnp.log2(1.0 + jnp.exp2(jnp.minimum(z, EXP2_SAFE))))


def _proj_kernel(x_ref, xprev_ref, g_ref, w_ref, cw_ref, gconv_ref,
                 k_ref, v_ref, qb_ref, kb_ref, vb_ref, sn_ref, tail_ref, cu_scr, *, q_scale):
    i = pl.program_id(1)
    tm = x_ref.shape[1]
    a = k_ref.shape[2]
    c = sn_ref.shape[2]
    g = g_ref[...]
    h = _rms(x_ref[0], g).astype(BF16)

    qb_ref[0] = (_dot(h, w_ref[:, 0:a]) * q_scale).astype(BF16)
    k = _dot(h, w_ref[:, a:2 * a])
    k_ref[0] = k
    kb_ref[0] = k.astype(BF16)
    v = _dot(h, w_ref[:, 2 * a:3 * a])
    v_ref[0] = v
    vb_ref[0] = v.astype(BF16)

    o = 3 * a
    gb = _dot(h, w_ref[:, o:o + c])
    cu = _dot(h, w_ref[:, o + c:o + 2 * c]) * _dot(h, w_ref[:, o + 2 * c:o + 3 * c])
    hp = _rms(xprev_ref[0], g).astype(BF16)
    cup = _dot(hp, w_ref[:, o + c:o + 2 * c]) * _dot(hp, w_ref[:, o + 2 * c:o + 3 * c])
    cu_scr[0:SUBLANES, :] = jnp.where(i == 0, 0.0, cup)
    cu_scr[SUBLANES:SUBLANES + tm, :] = cu
    y = (cw_ref[0:1, :] * cu_scr[SUBLANES - 2:SUBLANES - 2 + tm, :]
         + cw_ref[1:2, :] * cu_scr[SUBLANES - 1:SUBLANES - 1 + tm, :]
         + cw_ref[2:3, :] * cu)
    sn_ref[0] = _rms(gb * y, gconv_ref[...]).astype(BF16)
    tail_ref[0, 0] = cu[tm - SUBLANES:tm, :]


def _prompt_proj(x, g_pre, w_in, conv_w, g_conv, *, a, tm):
    b, t, d = x.shape
    c = conv_w.shape[1]
    nt = t // tm
    rows8 = tm // SUBLANES
    const2 = lambda bi, i: (0, 0)
    tile = lambda width: pl.BlockSpec((1, tm, width), lambda bi, i: (bi, i, 0))
    out_shape = (
        jax.ShapeDtypeStruct((b, t, a), F32),
        jax.ShapeDtypeStruct((b, t, a), F32),
        jax.ShapeDtypeStruct((b, t, a), BF16),
        jax.ShapeDtypeStruct((b, t, a), BF16),
        jax.ShapeDtypeStruct((b, t, a), BF16),
        jax.ShapeDtypeStruct((b, t, c), BF16),
        jax.ShapeDtypeStruct((b, nt, SUBLANES, c), F32),
    )
    return pl.pallas_call(
        functools.partial(_proj_kernel, q_scale=HEAD_DIM ** -0.5 * LOG2E),
        grid=(b, nt),
        in_specs=[
            tile(d),
            pl.BlockSpec((1, SUBLANES, d), lambda bi, i: (bi, jnp.maximum(i * rows8 - 1, 0), 0)),
            pl.BlockSpec(g_pre.shape, const2),
            pl.BlockSpec(w_in.shape, const2),
            pl.BlockSpec(conv_w.shape, const2),
            pl.BlockSpec(g_conv.shape, const2),
        ],
        out_specs=(tile(a), tile(a), tile(a), tile(a), tile(a), tile(c),
                   pl.BlockSpec((1, 1, SUBLANES, c), lambda bi, i: (bi, i, 0, 0))),
        out_shape=out_shape,
        scratch_shapes=[pltpu.VMEM((tm + SUBLANES, c), F32)],
        compiler_params=pltpu.CompilerParams(
            dimension_semantics=("parallel", "arbitrary"), vmem_limit_bytes=VMEM_LIMIT_BYTES),
        name="prompt_proj",
    )(x, x, g_pre, w_in, conv_w, g_conv)


def _attn_kernel(bias_ref, q_ref, k_ref, v_ref, o_ref, q_scr, tri_scr, carry_scr, acc_scr, *, tk, span_widths):
    _attn_step(pl.program_id(1), pl.program_id(2), bias_ref, q_ref, k_ref, v_ref, o_ref,
               q_scr, tri_scr, carry_scr, acc_scr, tk=tk, span_widths=span_widths)


def _attn_step(hp, i, bias_ref, q_ref, k_ref, v_ref, o_ref, q_scr, tri_scr, carry_scr, acc_scr,
               *, tk, span_widths, diag_hooks=()):
    tq, pair = q_ref.shape[1], q_ref.shape[2]
    n_heads_blk = pair // HEAD_DIM

    @pl.when(i == 0)
    def _():
        r = lax.broadcasted_iota(jnp.int32, (tk, tk), 0)
        cidx = lax.broadcasted_iota(jnp.int32, (tk, tk), 1)
        tri_scr[...] = jnp.where(r >= cidx, 1.0, 0.0).astype(BF16)

    q = q_ref[0]
    lane = lax.broadcasted_iota(jnp.int32, (tq, pair), 1)
    for hh in range(n_heads_blk):
        in_head = (lane >= hh * HEAD_DIM) & (lane < (hh + 1) * HEAD_DIM)
        q_scr[hh] = jnp.where(in_head, q, jnp.zeros_like(q))
    acc_scr[...] = jnp.zeros_like(acc_scr)
    carry_scr[...] = jnp.zeros_like(carry_scr)
    biases = [bias_ref[n_heads_blk * hp + hh] * LOG2E for hh in range(n_heads_blk)]

    def span(j_lo, n_sub, mask, hooks=()):
        ks = pl.ds(pl.multiple_of(j_lo * tk, tk), n_sub * tk)
        kb = k_ref[0, ks, :]
        vb = v_ref[0, ks, :]
        heads = range(n_heads_blk)
        carries = [carry_scr[hh] for hh in heads]
        accs = [acc_scr[hh] for hh in heads]
        zs, pbs = {}, {}

        def logits(s):
            rows = slice(s * tk, (s + 1) * tk)
            for hh in heads:
                zs[s, hh] = _dot_nt(q_scr[hh], kb[rows, :]) + biases[hh]

        def neg_log(s):
            for hh in heads:
                p = _neg_log2_one_minus_beta(zs[s, hh])
                if mask is not None:
                    p = jnp.where(mask, p, 0.0)
                pbs[s, hh] = p.astype(BF16)

        def weigh(s):
            rows = slice(s * tk, (s + 1) * tk)
            for hh in heads:
                incl = _dot(pbs.pop((s, hh)), tri_scr[...])
                carry = jnp.concatenate([carries[hh]] * (tk // LANES), axis=1)
                w = jnp.exp2(zs.pop((s, hh)) - incl - carry)
                if mask is not None:
                    w = jnp.where(mask, w, 0.0)
                accs[hh] = accs[hh] + _dot(w.astype(BF16), vb[rows, :])
                carries[hh] = carries[hh] + incl[:, 0:1]

        order = list(reversed(range(n_sub)))
        if hooks:
            hooks[0]()
        for step in range(n_sub + 2):
            if step < n_sub:
                logits(order[step])
            if 1 <= step <= n_sub:
                neg_log(order[step - 1])
            if step >= 2:
                weigh(order[step - 2])
            if step + 1 < len(hooks):
                hooks[step + 1]()
        for hh in heads:
            acc_scr[hh] = accs[hh]
            carry_scr[hh] = carries[hh]

    qr = lax.broadcasted_iota(jnp.int32, (tq, tk), 0)
    kc = lax.broadcasted_iota(jnp.int32, (tq, tk), 1)
    span(i, 1, kc < qr, diag_hooks)

    left = i
    for width in span_widths:
        n_steps = left // width

        def body(s, _, hi=left, width=width):
            span(hi - (s + 1) * width, width, None)
            return 0
        lax.fori_loop(0, n_steps, body, 0)
        left = left - n_steps * width

    out = acc_scr[0]
    for hh in range(1, n_heads_blk):
        out = jnp.where(lane >= hh * HEAD_DIM, acc_scr[hh], out)
    o_ref[0] = out


def _prompt_attn(bias, qb, kb, vb, *, tq):
    b, t, a = qb.shape
    tk = tq
    pair = LANES
    return pl.pallas_call(
        functools.partial(_attn_kernel, tk=tk, span_widths=ATTN_SPAN_WIDTHS),
        grid=(b, a // pair, t // tq),
        in_specs=[
            pl.BlockSpec(memory_space=pltpu.SMEM),
            pl.BlockSpec((1, tq, pair), lambda bi, hp, i: (bi, i, hp)),
            pl.BlockSpec((1, t, pair), lambda bi, hp, i: (bi, 0, hp)),
            pl.BlockSpec((1, t, pair), lambda bi, hp, i: (bi, 0, hp)),
        ],
        out_specs=pl.BlockSpec((1, tq, pair), lambda bi, hp, i: (bi, i, hp)),
        out_shape=jax.ShapeDtypeStruct((b, t, a), F32),
        scratch_shapes=[
            pltpu.VMEM((pair // HEAD_DIM, tq, pair), BF16),
            pltpu.VMEM((tk, tk), BF16),
            pltpu.VMEM((pair // HEAD_DIM, tq, LANES), F32),
            pltpu.VMEM((pair // HEAD_DIM, tq, pair), F32),
        ],
        compiler_params=pltpu.CompilerParams(
            dimension_semantics=("parallel", "parallel", "arbitrary"),
            vmem_limit_bytes=VMEM_LIMIT_BYTES),
        name="prompt_attn",
    )(bias, qb, kb, vb)


def _mix_mlp_kernel(x_ref, a_ref, sn_ref, ga_ref, gpm_ref, gpre_ref, gpost_ref,
                    woa_ref, wos_ref, wup_ref, wdn_ref, y_ref, *, ff_chunk):
    an = _rms(a_ref[...], ga_ref[...]).astype(BF16)
    mix = _dot(an, woa_ref[...]) + _dot(sn_ref[...], wos_ref[...])
    x1 = x_ref[...] + _rms(mix, gpm_ref[...])
    hm = _rms(x1, gpre_ref[...]).astype(BF16)
    m = jnp.zeros_like(x1)
    for j in range(wup_ref.shape[1] // ff_chunk):
        up = jnp.maximum(_dot(hm, wup_ref[:, j * ff_chunk:(j + 1) * ff_chunk]), 0.0)
        m = m + _dot((up * up).astype(BF16), wdn_ref[j * ff_chunk:(j + 1) * ff_chunk, :])
    y_ref[...] = x1 + _rms(m, gpost_ref[...])


def _mix_mlp(x, a, sn, g_attn, g_post_mix, g_pre_mlp, g_post_mlp, wo_a, wo_s, w_up, w_dn, *, tm):
    n, d = x.shape
    const = lambda i: (0, 0)
    resident = lambda arr: pl.BlockSpec(arr.shape, const, pipeline_mode=pl.Buffered(1))
    return pl.pallas_call(
        functools.partial(_mix_mlp_kernel, ff_chunk=min(1024, w_up.shape[1])),
        grid=(n // tm,),
        in_specs=[
            pl.BlockSpec((tm, d), lambda i: (i, 0)),
            pl.BlockSpec((tm, a.shape[1]), lambda i: (i, 0)),
            pl.BlockSpec((tm, sn.shape[1]), lambda i: (i, 0)),
            resident(g_attn), resident(g_post_mix), resident(g_pre_mlp), resident(g_post_mlp),
            resident(wo_a), resident(wo_s), resident(w_up), resident(w_dn),
        ],
        out_specs=pl.BlockSpec((tm, d), lambda i: (i, 0)),
        out_shape=jax.ShapeDtypeStruct((n, d), F32),
        compiler_params=pltpu.CompilerParams(
            dimension_semantics=("parallel",), vmem_limit_bytes=VMEM_LIMIT_BYTES),
        name="mix_mlp",
    )(x, a, sn, g_attn, g_post_mix, g_pre_mlp, g_post_mlp, wo_a, wo_s, w_up, w_dn)


def _sample_proj_kernel(x_ref, st0_ref, st1_ref, g_ref, w_ref, cw_ref, gconv_ref,
                        q_ref, k_ref, v_ref, cu_ref, sn_ref, *, scale):
    a = k_ref.shape[1]
    c = sn_ref.shape[1]
    h = _rms(x_ref[...], g_ref[...]).astype(BF16)
    q_ref[...] = _dot(h, w_ref[:, 0:a]) * scale
    k_ref[...] = _dot(h, w_ref[:, a:2 * a])
    v_ref[...] = _dot(h, w_ref[:, 2 * a:3 * a])
    gb = _dot(h, w_ref[:, 3 * a:3 * a + c])
    cu = _dot(h, w_ref[:, 3 * a + c:3 * a + 2 * c]) * _dot(h, w_ref[:, 3 * a + 2 * c:3 * a + 3 * c])
    cu_ref[...] = cu
    y = cw_ref[0:1, :] * st0_ref[...] + cw_ref[1:2, :] * st1_ref[...] + cw_ref[2:3, :] * cu
    sn_ref[...] = _rms(gb * y, gconv_ref[...]).astype(BF16)


def _sample_proj(x, st0, st1, g_pre, w_in, conv_w, g_conv, *, a):
    n = x.shape[0]
    c = conv_w.shape[1]
    return pl.pallas_call(
        functools.partial(_sample_proj_kernel, scale=HEAD_DIM ** -0.5 * LOG2E),
        out_shape=(
            jax.ShapeDtypeStruct((n, a), F32), jax.ShapeDtypeStruct((n, a), F32),
            jax.ShapeDtypeStruct((n, a), F32), jax.ShapeDtypeStruct((n, c), F32),
            jax.ShapeDtypeStruct((n, c), BF16),
        ),
        compiler_params=pltpu.CompilerParams(vmem_limit_bytes=VMEM_LIMIT_BYTES),
        name="sample_proj",
    )(x, st0, st1, g_pre, w_in, conv_w, g_conv)


def _paged_attn_kernel(pt_ref, q_ref, knew_ref, vnew_ref, bias_ref, *rest, n_pg):
    del pt_ref
    k_refs = rest[:n_pg]
    v_refs = rest[n_pg:2 * n_pg]
    o_ref = rest[2 * n_pg]
    scratch = rest[2 * n_pg + 1:]
    first, logits, weigh, values, last = _paged_attn_stages(
        pl.program_id(1), pl.num_programs(1), q_ref, knew_ref, vnew_ref, bias_ref,
        [r.at[0] for r in k_refs], [r.at[0] for r in v_refs], o_ref, *scratch)
    first()
    logits()
    weigh()
    values()
    last()


def _paged_attn_stages(cstep, n_steps, q_ref, knew_ref, vnew_ref, bias_ref, k_pages, v_pages, o_ref,
                       qbd_scr, tri_scr, carry_scr, acc_scr):
    n_pg = len(k_pages)
    n_heads, hd, page = k_pages[0].shape
    a = n_heads * hd
    bias = bias_ref[...]

    def weights(z, carry, mask):
        p = _neg_log2_one_minus_beta(z)
        if mask is not None:
            p = jnp.where(mask, p, 0.0)
        incl = _dot(p.astype(BF16), tri_scr[...])
        w = jnp.exp2(z - incl - carry)
        if mask is not None:
            w = jnp.where(mask, w, 0.0)
        return w, incl[:, 0:1]

    def first():
        pl.when(cstep == 0)(start_sequence)

    def start_sequence():
        sub = lax.broadcasted_iota(jnp.int32, (n_heads, a), 0)
        lane = lax.broadcasted_iota(jnp.int32, (n_heads, a), 1)
        q_all = jnp.concatenate([q_ref[0]] * n_heads, axis=1)
        qbd_scr[...] = jnp.where(lane // hd == sub, q_all, 0.0).astype(BF16)
        r = lax.broadcasted_iota(jnp.int32, (page, page), 0)
        c = lax.broadcasted_iota(jnp.int32, (page, page), 1)
        tri_scr[...] = jnp.where(r >= c, 1.0, 0.0).astype(BF16)
        n_new = knew_ref.shape[2]
        slot = lax.broadcasted_iota(jnp.int32, (n_heads, page), 1)
        q_off = 0
        visible = (slot < q_off) & (slot < n_new)
        k_new = jnp.broadcast_to(knew_ref[0][:, 0:1], (a, page)).astype(BF16)
        v_new = jnp.broadcast_to(vnew_ref[0][:, 0:1], (a, page))
        z = _dot(qbd_scr[...], k_new) + bias
        w, tot = weights(z, jnp.zeros((n_heads, page), F32), visible)
        carry_scr[...] = jnp.broadcast_to(tot, (n_heads, page))
        w_rows = jnp.concatenate([jnp.broadcast_to(w[h:h + 1, :], (hd, page)) for h in range(n_heads)], axis=0)
        acc_scr[...] = w_rows * v_new

    state = {}

    def logits():
        qbd = qbd_scr[...]
        state["zs"] = [_dot(qbd, k_pages[r][...].reshape(a, page).astype(BF16)) + bias for r in range(n_pg)]

    def weigh():
        z = jnp.concatenate(state["zs"], axis=0)
        p = _neg_log2_one_minus_beta(z)
        incl = _dot(p.astype(BF16), tri_scr[...])
        carry = carry_scr[...]
        ws = [None] * n_pg
        for r in reversed(range(n_pg)):
            rows = slice(r * n_heads, (r + 1) * n_heads)
            ws[r] = jnp.exp2(state["zs"][r] - incl[rows] - carry)
            carry = carry + incl[rows, 0:1]
        carry_scr[...] = carry
        state["ws"] = ws

    def values():
        for h in range(n_heads):
            rows = slice(h * hd, (h + 1) * hd)
            acc = acc_scr[rows, :]
            for r in range(n_pg):
                acc = acc + jnp.broadcast_to(state["ws"][r][h:h + 1, :], (hd, page)) * v_pages[r][h]
            acc_scr[rows, :] = acc

    def last():
        pl.when(cstep == n_steps - 1)(end_sequence)

    def end_sequence():
        o_ref[0] = jnp.sum(acc_scr[...], axis=1, keepdims=True)

    return first, logits, weigh, values, last


def _paged_attn(page_table, q, k_new, v_new, bias, cache_kt, cache_vt, *, n_pg):
    n, n_heads, hd = q.shape
    n_pages = page_table.shape[1]
    page = cache_kt.shape[3]
    a = n_heads * hd
    assert hd == HEAD_DIM
    n_steps = n_pages // n_pg
    bias_tile = jnp.broadcast_to(bias[:, None], (n_heads, page))

    def page_map(bi, c, pt, *, r):
        return (pt[bi * n_pages + (n_steps - 1 - c) * n_pg + r], 0, 0, 0)

    page_specs = [pl.BlockSpec((1, n_heads, hd, page), functools.partial(page_map, r=r))
                  for r in range(n_pg)]
    new_spec = pl.BlockSpec((1,) + k_new.shape[1:], lambda bi, c, pt: (bi, 0, 0))
    grid_spec = pltpu.PrefetchScalarGridSpec(
        num_scalar_prefetch=1,
        grid=(n, n_steps),
        in_specs=[pl.BlockSpec((1, n_heads, hd), lambda bi, c, pt: (bi, 0, 0)), new_spec, new_spec,
                  pl.BlockSpec((n_heads, page), lambda bi, c, pt: (0, 0))] + page_specs + page_specs,
        out_specs=pl.BlockSpec((1, a, 1), lambda bi, c, pt: (bi, 0, 0)),
        scratch_shapes=[
            pltpu.VMEM((n_heads, a), BF16),
            pltpu.VMEM((page, page), BF16),
            pltpu.VMEM((n_heads, page), F32),
            pltpu.VMEM((a, page), F32),
        ],
    )
    out = pl.pallas_call(
        functools.partial(_paged_attn_kernel, n_pg=n_pg),
        grid_spec=grid_spec,
        out_shape=jax.ShapeDtypeStruct((n, a, 1), F32),
        compiler_params=pltpu.CompilerParams(
            dimension_semantics=("parallel", "arbitrary"), vmem_limit_bytes=VMEM_LIMIT_BYTES),
        name="sample_attn",
    )(page_table.reshape(-1), q, k_new, v_new, bias_tile,
      *([cache_kt] * n_pg), *([cache_vt] * n_pg))
    return out[:, :, 0]


def _both_attn_kernel(pt_ref, bias_ref, q_ref, k_ref, v_ref, sq_ref, knew_ref, vnew_ref, sbias_ref,
                      cache_k_ref, cache_v_ref, o_ref, so_ref,
                      q_scr, tri_scr, carry_scr, acc_scr, sqbd_scr, stri_scr, scarry_scr, sacc_scr,
                      kbuf, vbuf, sem, *, n_pages, sample_steps, tk, span_widths):
    n_pg = kbuf.shape[1]
    hp, i = pl.program_id(1), pl.program_id(2)
    n_grid = pl.num_programs(0) * pl.num_programs(1) * pl.num_programs(2)
    step = (pl.program_id(0) * pl.num_programs(1) + hp) * pl.num_programs(2) + i
    slot = step & 1

    def page_copies(to_slot, s=None):
        if s is not None:
            seq_idx = _floor_div(s, sample_steps)
            first_page = seq_idx * n_pages + (sample_steps - 1 - (s - seq_idx * sample_steps)) * n_pg
        copies = []
        for r in range(n_pg):
            phys = 0 if s is None else pt_ref[first_page + r]
            copies.append(pltpu.make_async_copy(cache_k_ref.at[phys], kbuf.at[to_slot, r], sem.at[to_slot, 0]))
            copies.append(pltpu.make_async_copy(cache_v_ref.at[phys], vbuf.at[to_slot, r], sem.at[to_slot, 1]))
        return copies

    @pl.when(step == 0)
    def _():
        for c in page_copies(slot, step):
            c.start()

    @pl.when(step + 1 < n_grid)
    def _():
        for c in page_copies(1 - slot, step + 1):
            c.start()

    for c in page_copies(slot):
        c.wait()

    first, logits, weigh, values, last = _paged_attn_stages(
        step - _floor_div(step, sample_steps) * sample_steps, sample_steps,
        sq_ref, knew_ref, vnew_ref, sbias_ref,
        [kbuf.at[slot, r] for r in range(n_pg)], [vbuf.at[slot, r] for r in range(n_pg)], so_ref,
        sqbd_scr, stri_scr, scarry_scr, sacc_scr)
    first()
    _attn_step(hp, i, bias_ref, q_ref, k_ref, v_ref, o_ref, q_scr, tri_scr, carry_scr, acc_scr,
               tk=tk, span_widths=span_widths, diag_hooks=(logits, weigh, values))
    last()


def _both_attn(bias, qb, kb, vb, page_table, sq, k_new, v_new, cache_kt, cache_vt, *, tq, n_pg):
    b, t, a = qb.shape
    tk = tq
    pair = LANES
    n, n_heads, hd = sq.shape
    n_pages = page_table.shape[1]
    page = cache_kt.shape[3]
    n_hp, n_q = a // pair, t // tq
    sample_steps = n_pages // n_pg
    assert n * sample_steps == b * n_hp * n_q and hd == HEAD_DIM
    sbias = jnp.broadcast_to((bias * LOG2E)[:, None], (n_heads, page))

    def seq(bi, hp, i):
        return _floor_div((bi * n_hp + hp) * n_q + i, sample_steps)

    new_spec = pl.BlockSpec((1,) + k_new.shape[1:], lambda bi, hp, i, pt: (seq(bi, hp, i), 0, 0))
    grid_spec = pltpu.PrefetchScalarGridSpec(
        num_scalar_prefetch=1,
        grid=(b, n_hp, n_q),
        in_specs=[
            pl.BlockSpec(memory_space=pltpu.SMEM),
            pl.BlockSpec((1, tq, pair), lambda bi, hp, i, pt: (bi, i, hp)),
            pl.BlockSpec((1, t, pair), lambda bi, hp, i, pt: (bi, 0, hp)),
            pl.BlockSpec((1, t, pair), lambda bi, hp, i, pt: (bi, 0, hp)),
            pl.BlockSpec((1, n_heads, hd), lambda bi, hp, i, pt: (seq(bi, hp, i), 0, 0)),
            new_spec, new_spec,
            pl.BlockSpec((n_heads, page), lambda bi, hp, i, pt: (0, 0)),
            pl.BlockSpec(memory_space=pl.ANY),
            pl.BlockSpec(memory_space=pl.ANY),
        ],
        out_specs=(
            pl.BlockSpec((1, tq, pair), lambda bi, hp, i, pt: (bi, i, hp)),
            pl.BlockSpec((1, n_heads * hd, 1), lambda bi, hp, i, pt: (seq(bi, hp, i), 0, 0)),
        ),
        scratch_shapes=[
            pltpu.VMEM((pair // HEAD_DIM, tq, pair), BF16),
            pltpu.VMEM((tk, tk), BF16),
            pltpu.VMEM((pair // HEAD_DIM, tq, LANES), F32),
            pltpu.VMEM((pair // HEAD_DIM, tq, pair), F32),
            pltpu.VMEM((n_heads, n_heads * hd), BF16),
            pltpu.VMEM((page, page), BF16),
            pltpu.VMEM((n_heads, page), F32),
            pltpu.VMEM((n_heads * hd, page), F32),
            pltpu.VMEM((2, n_pg, n_heads, hd, page), F32),
            pltpu.VMEM((2, n_pg, n_heads, hd, page), F32),
            pltpu.SemaphoreType.DMA((2, 2)),
        ],
    )
    a_p, a_s = pl.pallas_call(
        functools.partial(_both_attn_kernel, n_pages=n_pages, sample_steps=sample_steps, tk=tk,
                          span_widths=ATTN_SPAN_WIDTHS),
        grid_spec=grid_spec,
        out_shape=(jax.ShapeDtypeStruct((b, t, a), F32), jax.ShapeDtypeStruct((n, n_heads * hd, 1), F32)),
        compiler_params=pltpu.CompilerParams(
            dimension_semantics=("arbitrary", "arbitrary", "arbitrary"), vmem_limit_bytes=VMEM_LIMIT_BYTES),
        name="both_attn",
    )(page_table.reshape(-1), bias, qb, kb, vb, sq, k_new, v_new, sbias, cache_kt, cache_vt)
    return a_p, a_s[:, :, 0]


def _pick_tile(n, want):
    t = min(n, want)
    while n % t:
        t //= 2
    return t


def _layer_step(xp, xs, cache_k, cache_v, state_conv, page_table,
                g_pre_mix, g_post_mix, g_pre_mlp, g_post_mlp,
                w_in, attn_bias, conv_w, g_attn, g_conv, w_out, w_up, w_down):
    b, t, d = xp.shape
    n_heads = attn_bias.shape[0]
    a = n_heads * HEAD_DIM
    c = conv_w.shape[1]
    row = lambda v: v.reshape(1, -1)
    g_pre_mix, g_post_mix, g_pre_mlp, g_post_mlp = map(row, (g_pre_mix, g_post_mix, g_pre_mlp, g_post_mlp))
    g_attn, g_conv = row(g_attn), row(g_conv)

    w_in_b = w_in.astype(BF16)
    wo_a = w_out[:a].astype(BF16)
    wo_s = w_out[a:].astype(BF16)
    w_up_b = w_up.astype(BF16)
    w_dn_b = w_down.astype(BF16)

    k_p, v_p, qb, kb, vb, sn_p, tails = _prompt_proj(
        xp, g_pre_mix, w_in_b, conv_w, g_conv, a=a, tm=_pick_tile(t, 512))
    n = xs.shape[0]
    xs2 = xs.reshape(n, d)
    q_s, k_s, v_s, cu_s, sn_s = _sample_proj(
        xs2, state_conv[:, 0, :], state_conv[:, 1, :], g_pre_mix, w_in_b, conv_w, g_conv, a=a)

    tq = _pick_tile(t, 256)
    n_pages = page_table.shape[1]
    n_pg = _pick_tile(n_pages, SAMPLE_PAGES_PER_STEP)
    to_keys_minor = lambda cache: jnp.transpose(cache, (0, 2, 3, 1))
    sample_args = (q_s.reshape(n, n_heads, HEAD_DIM), k_s[:, :, None], v_s[:, :, None])
    caches = (to_keys_minor(cache_k), to_keys_minor(cache_v))
    if n * (n_pages // n_pg) == b * (a // LANES) * (t // tq):
        a_p, a_s = _both_attn(attn_bias, qb, kb, vb, page_table, *sample_args, *caches, tq=tq, n_pg=n_pg)
    else:
        a_p = _prompt_attn(attn_bias, qb, kb, vb, tq=tq)
        a_s = _paged_attn(page_table, *sample_args, attn_bias * LOG2E, *caches, n_pg=n_pg)

    y_p = _mix_mlp(xp.reshape(b * t, d), a_p.reshape(b * t, a), sn_p.reshape(b * t, c),
                   g_attn, g_post_mix, g_pre_mlp, g_post_mlp, wo_a, wo_s, w_up_b, w_dn_b,
                   tm=_pick_tile(b * t, 512))
    y_s = _mix_mlp(xs2, a_s, sn_s, g_attn, g_post_mix, g_pre_mlp, g_post_mlp,
                   wo_a, wo_s, w_up_b, w_dn_b, tm=_pick_tile(n, 512))
    new_conv_p = tails[:, -1, SUBLANES - (CONV_TAPS - 1):, :]
    new_conv_s = jnp.stack([state_conv[:, 1, :], cu_s], axis=1)

    return (y_p.reshape(b, t, d), y_s.reshape(n, 1, d),
            k_p.reshape(b, t, n_heads, HEAD_DIM), v_p.reshape(b, t, n_heads, HEAD_DIM), new_conv_p,
            k_s.reshape(n, 1, n_heads, HEAD_DIM), v_s.reshape(n, 1, n_heads, HEAD_DIM), new_conv_s)


def kernel(x_prompt, x_sample, cache_k, cache_v, state_conv, page_table, norm_pre_mix, norm_post_mix,
           norm_pre_mlp, norm_post_mlp, w_in, attn_logit_bias, conv_w, norm_attn_out, norm_conv_out,
           w_out, w_up, w_down):
    depth = w_in.shape[0]
    assert x_sample.shape[1] == 1, "the sample path handles one new token per sequence"
    xp, xs = x_prompt, x_sample
    outs = []
    for l in range(depth):
        res = _layer_step(
            xp, xs, cache_k[l], cache_v[l], state_conv[l], page_table,
            norm_pre_mix[l], norm_post_mix[l], norm_pre_mlp[l], norm_post_mlp[l],
            w_in[l], attn_logit_bias[l], conv_w[l], norm_attn_out[l], norm_conv_out[l],
            w_out[l], w_up[l], w_down[l])
        xp, xs = res[0], res[1]
        outs.append(res[2:])
    stacked = tuple(jnp.stack([o[j] for o in outs]) for j in range(6))
    return (xp, xs) + stacked
```

```python
import functools

import jax
import jax.numpy as jnp
from jax import lax
from jax.experimental import pallas as pl
from jax.experimental.pallas import tpu as pltpu

HEAD_DIM = 64
CONV_TAPS = 3
EPS = 1e-6
SUBLANES = 8
LANES = 128
VMEM_LIMIT_BYTES = 56 * 1024 * 1024
LOG2E = 1.4426950408889634
EXP2_SAFE = 126.0
ATTN_SPAN_WIDTHS = (32, 16, 8, 4, 2, 1)
SAMPLE_PAGES_PER_STEP = 16

F32 = jnp.float32
BF16 = jnp.bfloat16


def _rms(x, g):
    return x * lax.rsqrt(jnp.mean(x * x, axis=-1, keepdims=True) + EPS) * g


def _floor_div(x, d):
    if d & (d - 1) == 0:
        return lax.shift_right_logical(x, jnp.int32(d.bit_length() - 1))
    return x // d


def _dot(a, b):
    return jnp.dot(a, b, preferred_element_type=F32)


def _dot_nt(a, b):
    return lax.dot_general(a, b, (((1,), (1,)), ((), ())), preferred_element_type=F32)


def _neg_log2_one_minus_beta(z):
    return jnp.maximum(z, jnp.log2(1.0 + jnp.exp2(jnp.minimum(z, EXP2_SAFE))))


def _proj_kernel(x_ref, xprev_ref, g_ref, w_ref, cw_ref, gconv_ref,
                 k_ref, v_ref, qb_ref, kb_ref, vb_ref, sn_ref, tail_ref, cu_scr, *, q_scale):
    i = pl.program_id(1)
    tm = x_ref.shape[1]
    a = k_ref.shape[2]
    c = sn_ref.shape[2]
    g = g_ref[...]
    h = _rms(x_ref[0], g).astype(BF16)

    qb_ref[0] = (_dot(h, w_ref[:, 0:a]) * q_scale).astype(BF16)
    k = _dot(h, w_ref[:, a:2 * a])
    k_ref[0] = k
    kb_ref[0] = k.astype(BF16)
    v = _dot(h, w_ref[:, 2 * a:3 * a])
    v_ref[0] = v
    vb_ref[0] = v.astype(BF16)

    o = 3 * a
    gb = _dot(h, w_ref[:, o:o + c])
    cu = _dot(h, w_ref[:, o + c:o + 2 * c]) * _dot(h, w_ref[:, o + 2 * c:o + 3 * c])
    hp = _rms(xprev_ref[0], g).astype(BF16)
    cup = _dot(hp, w_ref[:, o + c:o + 2 * c]) * _dot(hp, w_ref[:, o + 2 * c:o + 3 * c])
    cu_scr[0:SUBLANES, :] = jnp.where(i == 0, 0.0, cup)
    cu_scr[SUBLANES:SUBLANES + tm, :] = cu
    y = (cw_ref[0:1, :] * cu_scr[SUBLANES - 2:SUBLANES - 2 + tm, :]
         + cw_ref[1:2, :] * cu_scr[SUBLANES - 1:SUBLANES - 1 + tm, :]
         + cw_ref[2:3, :] * cu)
    sn_ref[0] = _rms(gb * y, gconv_ref[...]).astype(BF16)
    tail_ref[0, 0] = cu[tm - SUBLANES:tm, :]


def _prompt_proj(x, g_pre, w_in, conv_w, g_conv, *, a, tm):
    b, t, d = x.shape
    c = conv_w.shape[1]
    nt = t // tm
    rows8 = tm // SUBLANES
    const2 = lambda bi, i: (0, 0)
    tile = lambda width: pl.BlockSpec((1, tm, width), lambda bi, i: (bi, i, 0))
    out_shape = (
        jax.ShapeDtypeStruct((b, t, a), F32),
        jax.ShapeDtypeStruct((b, t, a), F32),
        jax.ShapeDtypeStruct((b, t, a), BF16),
        jax.ShapeDtypeStruct((b, t, a), BF16),
        jax.ShapeDtypeStruct((b, t, a), BF16),
        jax.ShapeDtypeStruct((b, t, c), BF16),
        jax.ShapeDtypeStruct((b, nt, SUBLANES, c), F32),
    )
    return pl.pallas_call(
        functools.partial(_proj_kernel, q_scale=HEAD_DIM ** -0.5 * LOG2E),
        grid=(b, nt),
        in_specs=[
            tile(d),
            pl.BlockSpec((1, SUBLANES, d), lambda bi, i: (bi, jnp.maximum(i * rows8 - 1, 0), 0)),
            pl.BlockSpec(g_pre.shape, const2),
            pl.BlockSpec(w_in.shape, const2),
            pl.BlockSpec(conv_w.shape, const2),
            pl.BlockSpec(g_conv.shape, const2),
        ],
        out_specs=(tile(a), tile(a), tile(a), tile(a), tile(a), tile(c),
                   pl.BlockSpec((1, 1, SUBLANES, c), lambda bi, i: (bi, i, 0, 0))),
        out_shape=out_shape,
        scratch_shapes=[pltpu.VMEM((tm + SUBLANES, c), F32)],
        compiler_params=pltpu.CompilerParams(
            dimension_semantics=("parallel", "arbitrary"), vmem_limit_bytes=VMEM_LIMIT_BYTES),
        name="prompt_proj",
    )(x, x, g_pre, w_in, conv_w, g_conv)


def _attn_kernel(bias_ref, q_ref, k_ref, v_ref, o_ref, q_scr, tri_scr, carry_scr, acc_scr, *, tk, span_widths):
    _attn_step(pl.program_id(1), pl.program_id(2), bias_ref, q_ref, k_ref, v_ref, o_ref,
               q_scr, tri_scr, carry_scr, acc_scr, tk=tk, span_widths=span_widths)


def _attn_scratch(tq, tk, pair):
    n_heads_blk = pair // HEAD_DIM
    return [
        pltpu.VMEM((n_heads_blk, tq, pair), BF16),
        pltpu.VMEM((tk, tk), BF16),
        pltpu.VMEM((n_heads_blk, tq, LANES), F32),
        pltpu.VMEM((n_heads_blk, tq, pair), F32),
    ]


def _attn_step(hp, i, bias_ref, q_ref, k_ref, v_ref, o_ref, q_scr, tri_scr, carry_scr, acc_scr,
               *, tk, span_widths, diag_hooks=()):
    tq, pair = q_ref.shape[1], q_ref.shape[2]
    n_heads_blk = pair // HEAD_DIM

    @pl.when(i == 0)
    def _():
        r = lax.broadcasted_iota(jnp.int32, (tk, tk), 0)
        cidx = lax.broadcasted_iota(jnp.int32, (tk, tk), 1)
        tri_scr[...] = jnp.where(r >= cidx, 1.0, 0.0).astype(BF16)

    q = q_ref[0]
    lane = lax.broadcasted_iota(jnp.int32, (tq, pair), 1)
    for hh in range(n_heads_blk):
        in_head = (lane >= hh * HEAD_DIM) & (lane < (hh + 1) * HEAD_DIM)
        q_scr[hh] = jnp.where(in_head, q, jnp.zeros_like(q))
    acc_scr[...] = jnp.zeros_like(acc_scr)
    carry_scr[...] = jnp.zeros_like(carry_scr)
    biases = [bias_ref[n_heads_blk * hp + hh] * LOG2E for hh in range(n_heads_blk)]

    def span(j_lo, n_sub, mask, hooks=()):
        ks = pl.ds(pl.multiple_of(j_lo * tk, tk), n_sub * tk)
        kb = k_ref[0, ks, :]
        vb = v_ref[0, ks, :]
        heads = range(n_heads_blk)
        carries = [carry_scr[hh] for hh in heads]
        accs = [acc_scr[hh] for hh in heads]
        zs, pbs = {}, {}

        def logits(s):
            rows = slice(s * tk, (s + 1) * tk)
            for hh in heads:
                zs[s, hh] = _dot_nt(q_scr[hh], kb[rows, :]) + biases[hh]

        def neg_log(s):
            for hh in heads:
                p = _neg_log2_one_minus_beta(zs[s, hh])
                if mask is not None:
                    p = jnp.where(mask, p, 0.0)
                pbs[s, hh] = p.astype(BF16)

        def weigh(s):
            rows = slice(s * tk, (s + 1) * tk)
            for hh in heads:
                incl = _dot(pbs.pop((s, hh)), tri_scr[...])
                carry = jnp.concatenate([carries[hh]] * (tk // LANES), axis=1)
                w = jnp.exp2(zs.pop((s, hh)) - incl - carry)
                if mask is not None:
                    w = jnp.where(mask, w, 0.0)
                accs[hh] = accs[hh] + _dot(w.astype(BF16), vb[rows, :])
                carries[hh] = carries[hh] + incl[:, 0:1]

        order = list(reversed(range(n_sub)))
        if hooks:
            hooks[0]()
        for step in range(n_sub + 2):
            if step < n_sub:
                logits(order[step])
            if 1 <= step <= n_sub:
                neg_log(order[step - 1])
            if step >= 2:
                weigh(order[step - 2])
            if step + 1 < len(hooks):
                hooks[step + 1]()
        for hh in heads:
            acc_scr[hh] = accs[hh]
            carry_scr[hh] = carries[hh]

    qr = lax.broadcasted_iota(jnp.int32, (tq, tk), 0)
    kc = lax.broadcasted_iota(jnp.int32, (tq, tk), 1)
    span(i, 1, kc < qr, diag_hooks)

    left = i
    for width in span_widths:
        n_steps = left // width

        def body(s, _, hi=left, width=width):
            span(hi - (s + 1) * width, width, None)
            return 0
        lax.fori_loop(0, n_steps, body, 0)
        left = left - n_steps * width

    out = acc_scr[0]
    for hh in range(1, n_heads_blk):
        out = jnp.where(lane >= hh * HEAD_DIM, acc_scr[hh], out)
    o_ref[0] = out


def _prompt_attn(bias, qb, kb, vb, *, tq):
    b, t, a = qb.shape
    tk = tq
    pair = LANES
    return pl.pallas_call(
        functools.partial(_attn_kernel, tk=tk, span_widths=ATTN_SPAN_WIDTHS),
        grid=(b, a // pair, t // tq),
        in_specs=[
            pl.BlockSpec(memory_space=pltpu.SMEM),
            pl.BlockSpec((1, tq, pair), lambda bi, hp, i: (bi, i, hp)),
            pl.BlockSpec((1, t, pair), lambda bi, hp, i: (bi, 0, hp)),
            pl.BlockSpec((1, t, pair), lambda bi, hp, i: (bi, 0, hp)),
        ],
        out_specs=pl.BlockSpec((1, tq, pair), lambda bi, hp, i: (bi, i, hp)),
        out_shape=jax.ShapeDtypeStruct((b, t, a), F32),
        scratch_shapes=_attn_scratch(tq, tk, pair),
        compiler_params=pltpu.CompilerParams(
            dimension_semantics=("parallel", "parallel", "arbitrary"),
            vmem_limit_bytes=VMEM_LIMIT_BYTES),
        name="prompt_attn",
    )(bias, qb, kb, vb)


def _mix_mlp_kernel(x_ref, a_ref, sn_ref, ga_ref, gpm_ref, gpre_ref, gpost_ref,
                    woa_ref, wos_ref, wup_ref, wdn_ref, y_ref, *, ff_chunk):
    an = _rms(a_ref[...], ga_ref[...]).astype(BF16)
    mix = _dot(an, woa_ref[...]) + _dot(sn_ref[...], wos_ref[...])
    x1 = x_ref[...] + _rms(mix, gpm_ref[...])
    hm = _rms(x1, gpre_ref[...]).astype(BF16)
    m = jnp.zeros_like(x1)
    for j in range(wup_ref.shape[1] // ff_chunk):
        up = jnp.maximum(_dot(hm, wup_ref[:, j * ff_chunk:(j + 1) * ff_chunk]), 0.0)
        m = m + _dot((up * up).astype(BF16), wdn_ref[j * ff_chunk:(j + 1) * ff_chunk, :])
    y_ref[...] = x1 + _rms(m, gpost_ref[...])


def _mix_mlp(x, a, sn, g_attn, g_post_mix, g_pre_mlp, g_post_mlp, wo_a, wo_s, w_up, w_dn, *, tm):
    n, d = x.shape
    const = lambda i: (0, 0)
    resident = lambda arr: pl.BlockSpec(arr.shape, const, pipeline_mode=pl.Buffered(1))
    return pl.pallas_call(
        functools.partial(_mix_mlp_kernel, ff_chunk=min(1024, w_up.shape[1])),
        grid=(n // tm,),
        in_specs=[
            pl.BlockSpec((tm, d), lambda i: (i, 0)),
            pl.BlockSpec((tm, a.shape[1]), lambda i: (i, 0)),
            pl.BlockSpec((tm, sn.shape[1]), lambda i: (i, 0)),
            resident(g_attn), resident(g_post_mix), resident(g_pre_mlp), resident(g_post_mlp),
            resident(wo_a), resident(wo_s), resident(w_up), resident(w_dn),
        ],
        out_specs=pl.BlockSpec((tm, d), lambda i: (i, 0)),
        out_shape=jax.ShapeDtypeStruct((n, d), F32),
        compiler_params=pltpu.CompilerParams(
            dimension_semantics=("parallel",), vmem_limit_bytes=VMEM_LIMIT_BYTES),
        name="mix_mlp",
    )(x, a, sn, g_attn, g_post_mix, g_pre_mlp, g_post_mlp, wo_a, wo_s, w_up, w_dn)


def _sample_proj_kernel(x_ref, st0_ref, st1_ref, g_ref, w_ref, cw_ref, gconv_ref,
                        q_ref, k_ref, v_ref, cu_ref, sn_ref, *, scale):
    a = k_ref.shape[1]
    c = sn_ref.shape[1]
    h = _rms(x_ref[...], g_ref[...]).astype(BF16)
    q_ref[...] = _dot(h, w_ref[:, 0:a]) * scale
    k_ref[...] = _dot(h, w_ref[:, a:2 * a])
    v_ref[...] = _dot(h, w_ref[:, 2 * a:3 * a])
    gb = _dot(h, w_ref[:, 3 * a:3 * a + c])
    cu = _dot(h, w_ref[:, 3 * a + c:3 * a + 2 * c]) * _dot(h, w_ref[:, 3 * a + 2 * c:3 * a + 3 * c])
    cu_ref[...] = cu
    y = cw_ref[0:1, :] * st0_ref[...] + cw_ref[1:2, :] * st1_ref[...] + cw_ref[2:3, :] * cu
    sn_ref[...] = _rms(gb * y, gconv_ref[...]).astype(BF16)


def _sample_proj(x, st0, st1, g_pre, w_in, conv_w, g_conv, *, a):
    n = x.shape[0]
    c = conv_w.shape[1]
    return pl.pallas_call(
        functools.partial(_sample_proj_kernel, scale=HEAD_DIM ** -0.5 * LOG2E),
        out_shape=(
            jax.ShapeDtypeStruct((n, a), F32), jax.ShapeDtypeStruct((n, a), F32),
            jax.ShapeDtypeStruct((n, a), F32), jax.ShapeDtypeStruct((n, c), F32),
            jax.ShapeDtypeStruct((n, c), BF16),
        ),
        compiler_params=pltpu.CompilerParams(vmem_limit_bytes=VMEM_LIMIT_BYTES),
        name="sample_proj",
    )(x, st0, st1, g_pre, w_in, conv_w, g_conv)


def _paged_attn_kernel(pt_ref, q_ref, knew_ref, vnew_ref, bias_ref, *rest, n_pg):
    del pt_ref
    k_refs = rest[:n_pg]
    v_refs = rest[n_pg:2 * n_pg]
    o_ref = rest[2 * n_pg]
    scratch = rest[2 * n_pg + 1:]
    first, logits, weigh, values, last = _paged_attn_stages(
        pl.program_id(1), pl.num_programs(1), q_ref, knew_ref, vnew_ref, bias_ref,
        [r.at[0] for r in k_refs], [r.at[0] for r in v_refs], o_ref, *scratch)
    first()
    logits()
    weigh()
    values()
    last()


def _paged_attn_stages(cstep, n_steps, q_ref, knew_ref, vnew_ref, bias_ref, k_pages, v_pages, o_ref,
                       qbd_scr, tri_scr, carry_scr, acc_scr):
    n_pg = len(k_pages)
    n_heads, hd, page = k_pages[0].shape
    a = n_heads * hd
    bias = bias_ref[...]

    def weights(z, carry, mask):
        p = _neg_log2_one_minus_beta(z)
        if mask is not None:
            p = jnp.where(mask, p, 0.0)
        incl = _dot(p.astype(BF16), tri_scr[...])
        w = jnp.exp2(z - incl - carry)
        if mask is not None:
            w = jnp.where(mask, w, 0.0)
        return w, incl[:, 0:1]

    def first():
        pl.when(cstep == 0)(start_sequence)

    def start_sequence():
        sub = lax.broadcasted_iota(jnp.int32, (n_heads, a), 0)
        lane = lax.broadcasted_iota(jnp.int32, (n_heads, a), 1)
        q_all = jnp.broadcast_to(q_ref[0], (n_heads, a))
        qbd_scr[...] = jnp.where(lane // hd == sub, q_all, 0.0).astype(BF16)
        r = lax.broadcasted_iota(jnp.int32, (page, page), 0)
        c = lax.broadcasted_iota(jnp.int32, (page, page), 1)
        tri_scr[...] = jnp.where(r >= c, 1.0, 0.0).astype(BF16)
        n_new = knew_ref.shape[1]
        assert n_new == 1, "one new key per sequence"
        slot = lax.broadcasted_iota(jnp.int32, (n_heads, page), 1)
        q_off = 0
        visible = (slot < q_off) & (slot < n_new)
        k_new = jnp.broadcast_to(knew_ref[0], (page, a)).T.astype(BF16)
        v_new = jnp.broadcast_to(vnew_ref[0], (page, a)).T
        z = _dot(qbd_scr[...], k_new) + bias
        w, tot = weights(z, jnp.zeros((n_heads, page), F32), visible)
        carry_scr[...] = jnp.broadcast_to(tot, (n_heads, page))
        w_rows = jnp.concatenate([jnp.broadcast_to(w[h:h + 1, :], (hd, page)) for h in range(n_heads)], axis=0)
        acc_scr[...] = w_rows * v_new

    state = {}

    def logits():
        qbd = qbd_scr[...]
        state["zs"] = [_dot(qbd, k_pages[r][...].reshape(a, page).astype(BF16)) + bias for r in range(n_pg)]

    def weigh():
        z = jnp.concatenate(state["zs"], axis=0)
        p = _neg_log2_one_minus_beta(z)
        incl = _dot(p.astype(BF16), tri_scr[...])
        carry = carry_scr[...]
        ws = [None] * n_pg
        for r in reversed(range(n_pg)):
            rows = slice(r * n_heads, (r + 1) * n_heads)
            ws[r] = jnp.exp2(state["zs"][r] - incl[rows] - carry)
            carry = carry + incl[rows, 0:1]
        carry_scr[...] = carry
        state["ws"] = ws

    def values():
        for h in range(n_heads):
            rows = slice(h * hd, (h + 1) * hd)
            acc = acc_scr[rows, :]
            for r in range(n_pg):
                acc = acc + jnp.broadcast_to(state["ws"][r][h:h + 1, :], (hd, page)) * v_pages[r][h]
            acc_scr[rows, :] = acc

    def last():
        pl.when(cstep == n_steps - 1)(end_sequence)

    def end_sequence():
        col = jnp.sum(acc_scr[...], axis=1, keepdims=True)
        o_ref[0] = jnp.broadcast_to(col, (a, page)).T[0:1, :]

    return first, logits, weigh, values, last


def _paged_attn(page_table, q, k_new, v_new, bias, cache_kt, cache_vt, *, n_pg):
    n, _, a = q.shape
    n_pages = page_table.shape[1]
    n_heads, hd, page = cache_kt.shape[1:]
    assert hd == HEAD_DIM and a == n_heads * hd
    n_steps = n_pages // n_pg
    bias_tile = jnp.broadcast_to(bias[:, None], (n_heads, page))

    def page_map(bi, c, pt, *, r):
        return (pt[bi * n_pages + (n_steps - 1 - c) * n_pg + r], 0, 0, 0)

    page_specs = [pl.BlockSpec((1, n_heads, hd, page), functools.partial(page_map, r=r))
                  for r in range(n_pg)]
    new_spec = pl.BlockSpec((1,) + k_new.shape[1:], lambda bi, c, pt: (bi, 0, 0))
    grid_spec = pltpu.PrefetchScalarGridSpec(
        num_scalar_prefetch=1,
        grid=(n, n_steps),
        in_specs=[pl.BlockSpec((1, 1, a), lambda bi, c, pt: (bi, 0, 0)), new_spec, new_spec,
                  pl.BlockSpec((n_heads, page), lambda bi, c, pt: (0, 0))] + page_specs + page_specs,
        out_specs=pl.BlockSpec((1, 1, a), lambda bi, c, pt: (bi, 0, 0)),
        scratch_shapes=[
            pltpu.VMEM((n_heads, a), BF16),
            pltpu.VMEM((page, page), BF16),
            pltpu.VMEM((n_heads, page), F32),
            pltpu.VMEM((a, page), F32),
        ],
    )
    out = pl.pallas_call(
        functools.partial(_paged_attn_kernel, n_pg=n_pg),
        grid_spec=grid_spec,
        out_shape=jax.ShapeDtypeStruct((n, 1, a), F32),
        compiler_params=pltpu.CompilerParams(
            dimension_semantics=("parallel", "arbitrary"), vmem_limit_bytes=VMEM_LIMIT_BYTES),
        name="sample_attn",
    )(page_table.reshape(-1), q, k_new, v_new, bias_tile,
      *([cache_kt] * n_pg), *([cache_vt] * n_pg))
    return out[:, 0, :]


def _both_attn_kernel(pt_ref, bias_ref, q_ref, k_ref, v_ref, sq_ref, knew_ref, vnew_ref, sbias_ref,
                      cache_k_ref, cache_v_ref, o_ref, so_ref,
                      q_scr, tri_scr, carry_scr, acc_scr, sqbd_scr, stri_scr, scarry_scr, sacc_scr,
                      kbuf, vbuf, sem, *, n_pages, sample_steps, tk, span_widths):
    n_pg = kbuf.shape[1]
    hp, i = pl.program_id(1), pl.program_id(2)
    n_grid = pl.num_programs(0) * pl.num_programs(1) * pl.num_programs(2)
    step = (pl.program_id(0) * pl.num_programs(1) + hp) * pl.num_programs(2) + i
    slot = step & 1

    def page_copies(to_slot, s=None):
        if s is not None:
            seq_idx = _floor_div(s, sample_steps)
            first_page = seq_idx * n_pages + (sample_steps - 1 - (s - seq_idx * sample_steps)) * n_pg
        copies = []
        for r in range(n_pg):
            phys = 0 if s is None else pt_ref[first_page + r]
            copies.append(pltpu.make_async_copy(cache_k_ref.at[phys], kbuf.at[to_slot, r], sem.at[to_slot, 0]))
            copies.append(pltpu.make_async_copy(cache_v_ref.at[phys], vbuf.at[to_slot, r], sem.at[to_slot, 1]))
        return copies

    @pl.when(step == 0)
    def _():
        for c in page_copies(slot, step):
            c.start()

    @pl.when(step + 1 < n_grid)
    def _():
        for c in page_copies(1 - slot, step + 1):
            c.start()

    for c in page_copies(slot):
        c.wait()

    first, logits, weigh, values, last = _paged_attn_stages(
        step - _floor_div(step, sample_steps) * sample_steps, sample_steps,
        sq_ref, knew_ref, vnew_ref, sbias_ref,
        [kbuf.at[slot, r] for r in range(n_pg)], [vbuf.at[slot, r] for r in range(n_pg)], so_ref,
        sqbd_scr, stri_scr, scarry_scr, sacc_scr)
    first()
    _attn_step(hp, i, bias_ref, q_ref, k_ref, v_ref, o_ref, q_scr, tri_scr, carry_scr, acc_scr,
               tk=tk, span_widths=span_widths, diag_hooks=(logits, weigh, values))
    last()


def _both_attn(bias, qb, kb, vb, page_table, sq, k_new, v_new, cache_kt, cache_vt, *, tq, n_pg):
    b, t, a = qb.shape
    tk = tq
    pair = LANES
    n = sq.shape[0]
    n_pages = page_table.shape[1]
    n_heads, hd, page = cache_kt.shape[1:]
    n_hp, n_q = a // pair, t // tq
    sample_steps = n_pages // n_pg
    assert n * sample_steps == b * n_hp * n_q and hd == HEAD_DIM
    sbias = jnp.broadcast_to((bias * LOG2E)[:, None], (n_heads, page))

    def seq(bi, hp, i):
        return _floor_div((bi * n_hp + hp) * n_q + i, sample_steps)

    new_spec = pl.BlockSpec((1,) + k_new.shape[1:], lambda bi, hp, i, pt: (seq(bi, hp, i), 0, 0))
    grid_spec = pltpu.PrefetchScalarGridSpec(
        num_scalar_prefetch=1,
        grid=(b, n_hp, n_q),
        in_specs=[
            pl.BlockSpec(memory_space=pltpu.SMEM),
            pl.BlockSpec((1, tq, pair), lambda bi, hp, i, pt: (bi, i, hp)),
            pl.BlockSpec((1, t, pair), lambda bi, hp, i, pt: (bi, 0, hp)),
            pl.BlockSpec((1, t, pair), lambda bi, hp, i, pt: (bi, 0, hp)),
            pl.BlockSpec((1, 1, n_heads * hd), lambda bi, hp, i, pt: (seq(bi, hp, i), 0, 0)),
            new_spec, new_spec,
            pl.BlockSpec((n_heads, page), lambda bi, hp, i, pt: (0, 0)),
            pl.BlockSpec(memory_space=pl.ANY),
            pl.BlockSpec(memory_space=pl.ANY),
        ],
        out_specs=(
            pl.BlockSpec((1, tq, pair), lambda bi, hp, i, pt: (bi, i, hp)),
            pl.BlockSpec((1, 1, n_heads * hd), lambda bi, hp, i, pt: (seq(bi, hp, i), 0, 0)),
        ),
        scratch_shapes=_attn_scratch(tq, tk, pair) + [
            pltpu.VMEM((n_heads, n_heads * hd), BF16),
            pltpu.VMEM((page, page), BF16),
            pltpu.VMEM((n_heads, page), F32),
            pltpu.VMEM((n_heads * hd, page), F32),
            pltpu.VMEM((2, n_pg, n_heads, hd, page), F32),
            pltpu.VMEM((2, n_pg, n_heads, hd, page), F32),
            pltpu.SemaphoreType.DMA((2, 2)),
        ],
    )
    a_p, a_s = pl.pallas_call(
        functools.partial(_both_attn_kernel, n_pages=n_pages, sample_steps=sample_steps, tk=tk,
                          span_widths=ATTN_SPAN_WIDTHS),
        grid_spec=grid_spec,
        out_shape=(jax.ShapeDtypeStruct((b, t, a), F32), jax.ShapeDtypeStruct((n, 1, n_heads * hd), F32)),
        compiler_params=pltpu.CompilerParams(
            dimension_semantics=("arbitrary", "arbitrary", "arbitrary"), vmem_limit_bytes=VMEM_LIMIT_BYTES),
        name="both_attn",
    )(page_table.reshape(-1), bias, qb, kb, vb, sq, k_new, v_new, sbias, cache_kt, cache_vt)
    return a_p, a_s[:, 0, :]


def _pick_tile(n, want):
    t = min(n, want)
    while n % t:
        t //= 2
    return t


def _layer_step(xp, xs, cache_k, cache_v, state_conv, page_table,
                g_pre_mix, g_post_mix, g_pre_mlp, g_post_mlp,
                w_in, attn_bias, conv_w, g_attn, g_conv, w_out, w_up, w_down):
    b, t, d = xp.shape
    n_heads = attn_bias.shape[0]
    a = n_heads * HEAD_DIM
    c = conv_w.shape[1]
    row = lambda v: v.reshape(1, -1)
    g_pre_mix, g_post_mix, g_pre_mlp, g_post_mlp = map(row, (g_pre_mix, g_post_mix, g_pre_mlp, g_post_mlp))
    g_attn, g_conv = row(g_attn), row(g_conv)

    w_in_b = w_in.astype(BF16)
    wo_a = w_out[:a].astype(BF16)
    wo_s = w_out[a:].astype(BF16)
    w_up_b = w_up.astype(BF16)
    w_dn_b = w_down.astype(BF16)

    k_p, v_p, qb, kb, vb, sn_p, tails = _prompt_proj(
        xp, g_pre_mix, w_in_b, conv_w, g_conv, a=a, tm=_pick_tile(t, 512))
    n = xs.shape[0]
    xs2 = xs.reshape(n, d)
    q_s, k_s, v_s, cu_s, sn_s = _sample_proj(
        xs2, state_conv[:, 0, :], state_conv[:, 1, :], g_pre_mix, w_in_b, conv_w, g_conv, a=a)

    tq = _pick_tile(t, 256)
    n_pages = page_table.shape[1]
    n_pg = _pick_tile(n_pages, SAMPLE_PAGES_PER_STEP)
    to_keys_minor = lambda cache: jnp.transpose(cache, (0, 2, 3, 1))
    sample_args = (q_s[:, None, :], k_s[:, None, :], v_s[:, None, :])
    caches = (to_keys_minor(cache_k), to_keys_minor(cache_v))
    if n * (n_pages // n_pg) == b * (a // LANES) * (t // tq):
        a_p, a_s = _both_attn(attn_bias, qb, kb, vb, page_table, *sample_args, *caches, tq=tq, n_pg=n_pg)
    else:
        a_p = _prompt_attn(attn_bias, qb, kb, vb, tq=tq)
        a_s = _paged_attn(page_table, *sample_args, attn_bias * LOG2E, *caches, n_pg=n_pg)

    y_p = _mix_mlp(xp.reshape(b * t, d), a_p.reshape(b * t, a), sn_p.reshape(b * t, c),
                   g_attn, g_post_mix, g_pre_mlp, g_post_mlp, wo_a, wo_s, w_up_b, w_dn_b,
                   tm=_pick_tile(b * t, 512))
    y_s = _mix_mlp(xs2, a_s, sn_s, g_attn, g_post_mix, g_pre_mlp, g_post_mlp,
                   wo_a, wo_s, w_up_b, w_dn_b, tm=_pick_tile(n, 512))
    new_conv_p = tails[:, -1, SUBLANES - (CONV_TAPS - 1):, :]
    new_conv_s = jnp.stack([state_conv[:, 1, :], cu_s], axis=1)

    return (y_p.reshape(b, t, d), y_s.reshape(n, 1, d),
            k_p.reshape(b, t, n_heads, HEAD_DIM), v_p.reshape(b, t, n_heads, HEAD_DIM), new_conv_p,
            k_s.reshape(n, 1, n_heads, HEAD_DIM), v_s.reshape(n, 1, n_heads, HEAD_DIM), new_conv_s)


def kernel(x_prompt, x_sample, cache_k, cache_v, state_conv, page_table, norm_pre_mix, norm_post_mix,
           norm_pre_mlp, norm_post_mlp, w_in, attn_logit_bias, conv_w, norm_attn_out, norm_conv_out,
           w_out, w_up, w_down):
    depth = w_in.shape[0]
    assert x_sample.shape[1] == 1, "the sample path handles one new token per sequence"
    xp, xs = x_prompt, x_sample
    outs = []
    for l in range(depth):
        res = _layer_step(
            xp, xs, cache_k[l], cache_v[l], state_conv[l], page_table,
            norm_pre_mix[l], norm_post_mix[l], norm_pre_mlp[l], norm_post_mlp[l],
            w_in[l], attn_logit_bias[l], conv_w[l], norm_attn_out[l], norm_conv_out[l],
            w_out[l], w_up[l], w_down[l])
        xp, xs = res[0], res[1]
        outs.append(res[2:])
    stacked = tuple(jnp.stack([o[j] for o in outs]) for j in range(6))
    return (xp, xs) + stacked
```

```python
import functools

import jax
import jax.numpy as jnp
from jax import lax
from jax.experimental import pallas as pl
from jax.experimental.pallas import tpu as pltpu

HEAD_DIM = 64
CONV_TAPS = 3
EPS = 1e-6
SUBLANES = 8
LANES = 128
VMEM_LIMIT_BYTES = 56 * 1024 * 1024
LOG2E = 1.4426950408889634
EXP2_SAFE = 126.0
ATTN_QUERY_BLOCK = 512
ATTN_KEY_BLOCK = 256
ATTN_SPAN_WIDTHS = (16, 8, 4, 2, 1)
SAMPLE_PAGES_PER_STEP = 32

F32 = jnp.float32
BF16 = jnp.bfloat16


def _rms(x, g):
    return x * lax.rsqrt(jnp.mean(x * x, axis=-1, keepdims=True) + EPS) * g


def _floor_div(x, d):
    if d & (d - 1) == 0:
        return lax.shift_right_logical(x, jnp.int32(d.bit_length() - 1))
    return x // d


def _dot(a, b):
    return jnp.dot(a, b, preferred_element_type=F32)


def _dot_nt(a, b):
    return lax.dot_general(a, b, (((1,), (1,)), ((), ())), preferred_element_type=F32)


def _neg_log2_one_minus_beta(z):
    return jnp.maximum(z, jnp.log2(1.0 + jnp.exp2(jnp.minimum(z, EXP2_SAFE))))


def _proj_kernel(x_ref, xprev_ref, g_ref, w_ref, cw_ref, gconv_ref,
                 k_ref, v_ref, qb_ref, kb_ref, vb_ref, sn_ref, tail_ref, cu_scr, *, q_scale):
    i = pl.program_id(1)
    tm = x_ref.shape[1]
    a = k_ref.shape[2]
    c = sn_ref.shape[2]
    g = g_ref[...]
    h = _rms(x_ref[0], g).astype(BF16)

    qb_ref[0] = (_dot(h, w_ref[:, 0:a]) * q_scale).astype(BF16)
    k = _dot(h, w_ref[:, a:2 * a])
    k_ref[0] = k
    kb_ref[0] = k.astype(BF16)
    v = _dot(h, w_ref[:, 2 * a:3 * a])
    v_ref[0] = v
    vb_ref[0] = v.astype(BF16)

    o = 3 * a
    gb = _dot(h, w_ref[:, o:o + c])
    cu = _dot(h, w_ref[:, o + c:o + 2 * c]) * _dot(h, w_ref[:, o + 2 * c:o + 3 * c])
    hp = _rms(xprev_ref[0], g).astype(BF16)
    cup = _dot(hp, w_ref[:, o + c:o + 2 * c]) * _dot(hp, w_ref[:, o + 2 * c:o + 3 * c])
    cu_scr[0:SUBLANES, :] = jnp.where(i == 0, 0.0, cup)
    cu_scr[SUBLANES:SUBLANES + tm, :] = cu
    y = (cw_ref[0:1, :] * cu_scr[SUBLANES - 2:SUBLANES - 2 + tm, :]
         + cw_ref[1:2, :] * cu_scr[SUBLANES - 1:SUBLANES - 1 + tm, :]
         + cw_ref[2:3, :] * cu)
    sn_ref[0] = _rms(gb * y, gconv_ref[...]).astype(BF16)
    tail_ref[0, 0] = cu[tm - SUBLANES:tm, :]


def _prompt_proj(x, g_pre, w_in, conv_w, g_conv, *, a, tm):
    b, t, d = x.shape
    c = conv_w.shape[1]
    nt = t // tm
    rows8 = tm // SUBLANES
    const2 = lambda bi, i: (0, 0)
    tile = lambda width: pl.BlockSpec((1, tm, width), lambda bi, i: (bi, i, 0))
    out_shape = (
        jax.ShapeDtypeStruct((b, t, a), F32),
        jax.ShapeDtypeStruct((b, t, a), F32),
        jax.ShapeDtypeStruct((b, t, a), BF16),
        jax.ShapeDtypeStruct((b, t, a), BF16),
        jax.ShapeDtypeStruct((b, t, a), BF16),
        jax.ShapeDtypeStruct((b, t, c), BF16),
        jax.ShapeDtypeStruct((b, nt, SUBLANES, c), F32),
    )
    return pl.pallas_call(
        functools.partial(_proj_kernel, q_scale=HEAD_DIM ** -0.5 * LOG2E),
        grid=(b, nt),
        in_specs=[
            tile(d),
            pl.BlockSpec((1, SUBLANES, d), lambda bi, i: (bi, jnp.maximum(i * rows8 - 1, 0), 0)),
            pl.BlockSpec(g_pre.shape, const2),
            pl.BlockSpec(w_in.shape, const2),
            pl.BlockSpec(conv_w.shape, const2),
            pl.BlockSpec(g_conv.shape, const2),
        ],
        out_specs=(tile(a), tile(a), tile(a), tile(a), tile(a), tile(c),
                   pl.BlockSpec((1, 1, SUBLANES, c), lambda bi, i: (bi, i, 0, 0))),
        out_shape=out_shape,
        scratch_shapes=[pltpu.VMEM((tm + SUBLANES, c), F32)],
        compiler_params=pltpu.CompilerParams(
            dimension_semantics=("parallel", "arbitrary"), vmem_limit_bytes=VMEM_LIMIT_BYTES),
        name="prompt_proj",
    )(x, x, g_pre, w_in, conv_w, g_conv)


def _attn_kernel(bias_ref, q_ref, k_ref, v_ref, o_ref, q_scr, tri_scr, carry_scr, acc_scr, *, tk, span_widths):
    _attn_step(pl.program_id(1), pl.program_id(2), bias_ref, q_ref, k_ref, v_ref, o_ref,
               q_scr, tri_scr, carry_scr, acc_scr, tk=tk, span_widths=span_widths)


def _attn_scratch(tq, tk, pair):
    n_heads_blk = pair // HEAD_DIM
    return [
        pltpu.VMEM((n_heads_blk, tq, pair), BF16),
        pltpu.VMEM((tk, tk), BF16),
        pltpu.VMEM((n_heads_blk, tq, LANES), F32),
        pltpu.VMEM((n_heads_blk, tq, pair), F32),
    ]


def _attn_step(hp, i, bias_ref, q_ref, k_ref, v_ref, o_ref, q_scr, tri_scr, carry_scr, acc_scr,
               *, tk, span_widths, diag_hooks=()):
    tq, pair = q_ref.shape[1], q_ref.shape[2]
    n_heads_blk = pair // HEAD_DIM

    @pl.when(i == 0)
    def _():
        r = lax.broadcasted_iota(jnp.int32, (tk, tk), 0)
        cidx = lax.broadcasted_iota(jnp.int32, (tk, tk), 1)
        tri_scr[...] = jnp.where(r >= cidx, 1.0, 0.0).astype(BF16)

    q = q_ref[0]
    lane = lax.broadcasted_iota(jnp.int32, (tq, pair), 1)
    for hh in range(n_heads_blk):
        in_head = (lane >= hh * HEAD_DIM) & (lane < (hh + 1) * HEAD_DIM)
        q_scr[hh] = jnp.where(in_head, q, jnp.zeros_like(q))
    acc_scr[...] = jnp.zeros_like(acc_scr)
    carry_scr[...] = jnp.zeros_like(carry_scr)
    biases = [bias_ref[n_heads_blk * hp + hh] * LOG2E for hh in range(n_heads_blk)]

    def span(j_lo, n_sub, mask, hooks=()):
        ks = pl.ds(pl.multiple_of(j_lo * tk, tk), n_sub * tk)
        kb = k_ref[0, ks, :]
        vb = v_ref[0, ks, :]
        heads = range(n_heads_blk)
        carries = [carry_scr[hh] for hh in heads]
        accs = [acc_scr[hh] for hh in heads]
        zs, pbs = {}, {}

        def logits(s):
            rows = slice(s * tk, (s + 1) * tk)
            for hh in heads:
                zs[s, hh] = _dot_nt(q_scr[hh], kb[rows, :]) + biases[hh]

        def neg_log(s):
            for hh in heads:
                p = _neg_log2_one_minus_beta(zs[s, hh])
                if mask is not None:
                    p = jnp.where(mask(s), p, 0.0)
                pbs[s, hh] = p.astype(BF16)

        def weigh(s):
            rows = slice(s * tk, (s + 1) * tk)
            for hh in heads:
                incl = _dot(pbs.pop((s, hh)), tri_scr[...])
                carry = jnp.concatenate([carries[hh]] * (tk // LANES), axis=1)
                w = jnp.exp2(zs.pop((s, hh)) - incl - carry)
                if mask is not None:
                    w = jnp.where(mask(s), w, 0.0)
                accs[hh] = accs[hh] + _dot(w.astype(BF16), vb[rows, :])
                carries[hh] = carries[hh] + incl[:, 0:1]

        order = list(reversed(range(n_sub)))
        if hooks:
            hooks[0]()
        for step in range(n_sub + 2):
            if step < n_sub:
                logits(order[step])
            if 1 <= step <= n_sub:
                neg_log(order[step - 1])
            if step >= 2:
                weigh(order[step - 2])
            if step + 1 < len(hooks):
                hooks[step + 1]()
        for hh in heads:
            acc_scr[hh] = accs[hh]
            carry_scr[hh] = carries[hh]

    n_diag = tq // tk
    qr = lax.broadcasted_iota(jnp.int32, (tq, tk), 0)
    kc = lax.broadcasted_iota(jnp.int32, (tq, tk), 1)
    span(i * n_diag, n_diag, lambda s: kc + s * tk < qr, diag_hooks)

    left = i * n_diag
    for width in span_widths:
        n_steps = left // width

        def body(s, _, hi=left, width=width):
            span(hi - (s + 1) * width, width, None)
            return 0
        lax.fori_loop(0, n_steps, body, 0)
        left = left - n_steps * width

    out = acc_scr[0]
    for hh in range(1, n_heads_blk):
        out = jnp.where(lane >= hh * HEAD_DIM, acc_scr[hh], out)
    o_ref[0] = out


def _prompt_attn(bias, qb, kb, vb, *, tq, tk):
    b, t, a = qb.shape
    assert tq % tk == 0
    pair = LANES
    return pl.pallas_call(
        functools.partial(_attn_kernel, tk=tk, span_widths=ATTN_SPAN_WIDTHS),
        grid=(b, a // pair, t // tq),
        in_specs=[
            pl.BlockSpec(memory_space=pltpu.SMEM),
            pl.BlockSpec((1, tq, pair), lambda bi, hp, i: (bi, i, hp)),
            pl.BlockSpec((1, t, pair), lambda bi, hp, i: (bi, 0, hp)),
            pl.BlockSpec((1, t, pair), lambda bi, hp, i: (bi, 0, hp)),
        ],
        out_specs=pl.BlockSpec((1, tq, pair), lambda bi, hp, i: (bi, i, hp)),
        out_shape=jax.ShapeDtypeStruct((b, t, a), F32),
        scratch_shapes=_attn_scratch(tq, tk, pair),
        compiler_params=pltpu.CompilerParams(
            dimension_semantics=("parallel", "parallel", "arbitrary"),
            vmem_limit_bytes=VMEM_LIMIT_BYTES),
        name="prompt_attn",
    )(bias, qb, kb, vb)


def _mix_mlp_kernel(x_ref, a_ref, sn_ref, ga_ref, gpm_ref, gpre_ref, gpost_ref,
                    woa_ref, wos_ref, wup_ref, wdn_ref, y_ref, *, ff_chunk):
    an = _rms(a_ref[...], ga_ref[...]).astype(BF16)
    mix = _dot(an, woa_ref[...]) + _dot(sn_ref[...], wos_ref[...])
    x1 = x_ref[...] + _rms(mix, gpm_ref[...])
    hm = _rms(x1, gpre_ref[...]).astype(BF16)
    m = jnp.zeros_like(x1)
    for j in range(wup_ref.shape[1] // ff_chunk):
        up = jnp.maximum(_dot(hm, wup_ref[:, j * ff_chunk:(j + 1) * ff_chunk]), 0.0)
        m = m + _dot((up * up).astype(BF16), wdn_ref[j * ff_chunk:(j + 1) * ff_chunk, :])
    y_ref[...] = x1 + _rms(m, gpost_ref[...])


def _mix_mlp(x, a, sn, g_attn, g_post_mix, g_pre_mlp, g_post_mlp, wo_a, wo_s, w_up, w_dn, *, tm):
    n, d = x.shape
    const = lambda i: (0, 0)
    resident = lambda arr: pl.BlockSpec(arr.shape, const, pipeline_mode=pl.Buffered(1))
    return pl.pallas_call(
        functools.partial(_mix_mlp_kernel, ff_chunk=min(1024, w_up.shape[1])),
        grid=(n // tm,),
        in_specs=[
            pl.BlockSpec((tm, d), lambda i: (i, 0)),
            pl.BlockSpec((tm, a.shape[1]), lambda i: (i, 0)),
            pl.BlockSpec((tm, sn.shape[1]), lambda i: (i, 0)),
            resident(g_attn), resident(g_post_mix), resident(g_pre_mlp), resident(g_post_mlp),
            resident(wo_a), resident(wo_s), resident(w_up), resident(w_dn),
        ],
        out_specs=pl.BlockSpec((tm, d), lambda i: (i, 0)),
        out_shape=jax.ShapeDtypeStruct((n, d), F32),
        compiler_params=pltpu.CompilerParams(
            dimension_semantics=("parallel",), vmem_limit_bytes=VMEM_LIMIT_BYTES),
        name="mix_mlp",
    )(x, a, sn, g_attn, g_post_mix, g_pre_mlp, g_post_mlp, wo_a, wo_s, w_up, w_dn)


def _sample_proj_kernel(x_ref, st0_ref, st1_ref, g_ref, w_ref, cw_ref, gconv_ref,
                        q_ref, k_ref, v_ref, cu_ref, sn_ref, *, scale):
    a = k_ref.shape[1]
    c = sn_ref.shape[1]
    h = _rms(x_ref[...], g_ref[...]).astype(BF16)
    q_ref[...] = _dot(h, w_ref[:, 0:a]) * scale
    k_ref[...] = _dot(h, w_ref[:, a:2 * a])
    v_ref[...] = _dot(h, w_ref[:, 2 * a:3 * a])
    gb = _dot(h, w_ref[:, 3 * a:3 * a + c])
    cu = _dot(h, w_ref[:, 3 * a + c:3 * a + 2 * c]) * _dot(h, w_ref[:, 3 * a + 2 * c:3 * a + 3 * c])
    cu_ref[...] = cu
    y = cw_ref[0:1, :] * st0_ref[...] + cw_ref[1:2, :] * st1_ref[...] + cw_ref[2:3, :] * cu
    sn_ref[...] = _rms(gb * y, gconv_ref[...]).astype(BF16)


def _sample_proj(x, st0, st1, g_pre, w_in, conv_w, g_conv, *, a):
    n = x.shape[0]
    c = conv_w.shape[1]
    return pl.pallas_call(
        functools.partial(_sample_proj_kernel, scale=HEAD_DIM ** -0.5 * LOG2E),
        out_shape=(
            jax.ShapeDtypeStruct((n, a), F32), jax.ShapeDtypeStruct((n, a), F32),
            jax.ShapeDtypeStruct((n, a), F32), jax.ShapeDtypeStruct((n, c), F32),
            jax.ShapeDtypeStruct((n, c), BF16),
        ),
        compiler_params=pltpu.CompilerParams(vmem_limit_bytes=VMEM_LIMIT_BYTES),
        name="sample_proj",
    )(x, st0, st1, g_pre, w_in, conv_w, g_conv)


def _paged_attn_kernel(pt_ref, q_ref, knew_ref, vnew_ref, bias_ref, *rest, n_pg):
    del pt_ref
    k_refs = rest[:n_pg]
    v_refs = rest[n_pg:2 * n_pg]
    o_ref = rest[2 * n_pg]
    scratch = rest[2 * n_pg + 1:]
    first, logits, weigh, values, last = _paged_attn_stages(
        pl.program_id(1), pl.num_programs(1), q_ref, knew_ref, vnew_ref, bias_ref,
        [r.at[0] for r in k_refs], [r.at[0] for r in v_refs], o_ref, *scratch)
    first()
    logits()
    weigh()
    values()
    last()


def _paged_attn_stages(cstep, n_steps, q_ref, knew_ref, vnew_ref, bias_ref, k_pages, v_pages, o_ref,
                       qbd_scr, tri_scr, carry_scr, acc_scr):
    n_pg = len(k_pages)
    n_heads, hd, page = k_pages[0].shape
    a = n_heads * hd
    bias = bias_ref[...]

    def weights(z, carry, mask):
        p = _neg_log2_one_minus_beta(z)
        if mask is not None:
            p = jnp.where(mask, p, 0.0)
        incl = _dot(p.astype(BF16), tri_scr[...])
        w = jnp.exp2(z - incl - carry)
        if mask is not None:
            w = jnp.where(mask, w, 0.0)
        return w, incl[:, 0:1]

    def first():
        pl.when(cstep == 0)(start_sequence)

    def start_sequence():
        sub = lax.broadcasted_iota(jnp.int32, (n_heads, a), 0)
        lane = lax.broadcasted_iota(jnp.int32, (n_heads, a), 1)
        q_all = jnp.broadcast_to(q_ref[0], (n_heads, a))
        qbd_scr[...] = jnp.where(lane // hd == sub, q_all, 0.0).astype(BF16)
        r = lax.broadcasted_iota(jnp.int32, (page, page), 0)
        c = lax.broadcasted_iota(jnp.int32, (page, page), 1)
        tri_scr[...] = jnp.where(r >= c, 1.0, 0.0).astype(BF16)
        n_new = knew_ref.shape[1]
        assert n_new == 1, "one new key per sequence"
        slot = lax.broadcasted_iota(jnp.int32, (n_heads, page), 1)
        q_off = 0
        visible = (slot < q_off) & (slot < n_new)
        k_new = jnp.broadcast_to(knew_ref[0], (page, a)).T.astype(BF16)
        v_new = jnp.broadcast_to(vnew_ref[0], (page, a)).T
        z = _dot(qbd_scr[...], k_new) + bias
        w, tot = weights(z, jnp.zeros((n_heads, page), F32), visible)
        carry_scr[...] = jnp.broadcast_to(tot, (n_heads, page))
        w_rows = jnp.concatenate([jnp.broadcast_to(w[h:h + 1, :], (hd, page)) for h in range(n_heads)], axis=0)
        acc_scr[...] = w_rows * v_new

    state = {}

    def logits():
        qbd = qbd_scr[...]
        state["zs"] = [_dot(qbd, k_pages[r][...].reshape(a, page).astype(BF16)) + bias for r in range(n_pg)]

    def weigh():
        z = jnp.concatenate(state["zs"], axis=0)
        p = _neg_log2_one_minus_beta(z)
        incl = _dot(p.astype(BF16), tri_scr[...])
        carry = carry_scr[...]
        ws = [None] * n_pg
        for r in reversed(range(n_pg)):
            rows = slice(r * n_heads, (r + 1) * n_heads)
            ws[r] = jnp.exp2(state["zs"][r] - incl[rows] - carry)
            carry = carry + incl[rows, 0:1]
        carry_scr[...] = carry
        state["ws"] = ws

    def values():
        for h in range(n_heads):
            rows = slice(h * hd, (h + 1) * hd)
            acc = acc_scr[rows, :]
            for r in range(n_pg):
                acc = acc + jnp.broadcast_to(state["ws"][r][h:h + 1, :], (hd, page)) * v_pages[r][h]
            acc_scr[rows, :] = acc

    def last():
        pl.when(cstep == n_steps - 1)(end_sequence)

    def end_sequence():
        col = jnp.sum(acc_scr[...], axis=1, keepdims=True)
        o_ref[0] = jnp.broadcast_to(col, (a, page)).T[0:1, :]

    return first, logits, weigh, values, last


def _paged_attn(page_table, q, k_new, v_new, bias, cache_kt, cache_vt, *, n_pg):
    n, _, a = q.shape
    n_pages = page_table.shape[1]
    n_heads, hd, page = cache_kt.shape[1:]
    assert hd == HEAD_DIM and a == n_heads * hd
    n_steps = n_pages // n_pg
    bias_tile = jnp.broadcast_to(bias[:, None], (n_heads, page))

    def page_map(bi, c, pt, *, r):
        return (pt[bi * n_pages + (n_steps - 1 - c) * n_pg + r], 0, 0, 0)

    page_specs = [pl.BlockSpec((1, n_heads, hd, page), functools.partial(page_map, r=r))
                  for r in range(n_pg)]
    new_spec = pl.BlockSpec((1,) + k_new.shape[1:], lambda bi, c, pt: (bi, 0, 0))
    grid_spec = pltpu.PrefetchScalarGridSpec(
        num_scalar_prefetch=1,
        grid=(n, n_steps),
        in_specs=[pl.BlockSpec((1, 1, a), lambda bi, c, pt: (bi, 0, 0)), new_spec, new_spec,
                  pl.BlockSpec((n_heads, page), lambda bi, c, pt: (0, 0))] + page_specs + page_specs,
        out_specs=pl.BlockSpec((1, 1, a), lambda bi, c, pt: (bi, 0, 0)),
        scratch_shapes=[
            pltpu.VMEM((n_heads, a), BF16),
            pltpu.VMEM((page, page), BF16),
            pltpu.VMEM((n_heads, page), F32),
            pltpu.VMEM((a, page), F32),
        ],
    )
    out = pl.pallas_call(
        functools.partial(_paged_attn_kernel, n_pg=n_pg),
        grid_spec=grid_spec,
        out_shape=jax.ShapeDtypeStruct((n, 1, a), F32),
        compiler_params=pltpu.CompilerParams(
            dimension_semantics=("parallel", "arbitrary"), vmem_limit_bytes=VMEM_LIMIT_BYTES),
        name="sample_attn",
    )(page_table.reshape(-1), q, k_new, v_new, bias_tile,
      *([cache_kt] * n_pg), *([cache_vt] * n_pg))
    return out[:, 0, :]


def _both_attn_kernel(pt_ref, bias_ref, q_ref, k_ref, v_ref, sq_ref, knew_ref, vnew_ref, sbias_ref,
                      cache_k_ref, cache_v_ref, o_ref, so_ref,
                      q_scr, tri_scr, carry_scr, acc_scr, sqbd_scr, stri_scr, scarry_scr, sacc_scr,
                      kbuf, vbuf, sem, *, n_pages, sample_steps, tk, span_widths):
    n_pg = kbuf.shape[1]
    hp, i = pl.program_id(1), pl.program_id(2)
    n_grid = pl.num_programs(0) * pl.num_programs(1) * pl.num_programs(2)
    step = (pl.program_id(0) * pl.num_programs(1) + hp) * pl.num_programs(2) + i
    slot = step & 1

    def page_copies(to_slot, s=None):
        if s is not None:
            seq_idx = _floor_div(s, sample_steps)
            first_page = seq_idx * n_pages + (sample_steps - 1 - (s - seq_idx * sample_steps)) * n_pg
        copies = []
        for r in range(n_pg):
            phys = 0 if s is None else pt_ref[first_page + r]
            copies.append(pltpu.make_async_copy(cache_k_ref.at[phys], kbuf.at[to_slot, r], sem.at[to_slot, 0]))
            copies.append(pltpu.make_async_copy(cache_v_ref.at[phys], vbuf.at[to_slot, r], sem.at[to_slot, 1]))
        return copies

    @pl.when(step == 0)
    def _():
        for c in page_copies(slot, step):
            c.start()

    @pl.when(step + 1 < n_grid)
    def _():
        for c in page_copies(1 - slot, step + 1):
            c.start()

    for c in page_copies(slot):
        c.wait()

    first, logits, weigh, values, last = _paged_attn_stages(
        step - _floor_div(step, sample_steps) * sample_steps, sample_steps,
        sq_ref, knew_ref, vnew_ref, sbias_ref,
        [kbuf.at[slot, r] for r in range(n_pg)], [vbuf.at[slot, r] for r in range(n_pg)], so_ref,
        sqbd_scr, stri_scr, scarry_scr, sacc_scr)
    first()
    _attn_step(hp, i, bias_ref, q_ref, k_ref, v_ref, o_ref, q_scr, tri_scr, carry_scr, acc_scr,
               tk=tk, span_widths=span_widths, diag_hooks=(logits, weigh, values))
    last()


def _both_attn(bias, qb, kb, vb, page_table, sq, k_new, v_new, cache_kt, cache_vt, *, tq, tk, n_pg):
    b, t, a = qb.shape
    assert tq % tk == 0
    pair = LANES
    n = sq.shape[0]
    n_pages = page_table.shape[1]
    n_heads, hd, page = cache_kt.shape[1:]
    n_hp, n_q = a // pair, t // tq
    sample_steps = n_pages // n_pg
    assert n * sample_steps == b * n_hp * n_q and hd == HEAD_DIM
    sbias = jnp.broadcast_to((bias * LOG2E)[:, None], (n_heads, page))

    def seq(bi, hp, i):
        return _floor_div((bi * n_hp + hp) * n_q + i, sample_steps)

    new_spec = pl.BlockSpec((1,) + k_new.shape[1:], lambda bi, hp, i, pt: (seq(bi, hp, i), 0, 0))
    grid_spec = pltpu.PrefetchScalarGridSpec(
        num_scalar_prefetch=1,
        grid=(b, n_hp, n_q),
        in_specs=[
            pl.BlockSpec(memory_space=pltpu.SMEM),
            pl.BlockSpec((1, tq, pair), lambda bi, hp, i, pt: (bi, i, hp)),
            pl.BlockSpec((1, t, pair), lambda bi, hp, i, pt: (bi, 0, hp), pipeline_mode=pl.Buffered(1)),
            pl.BlockSpec((1, t, pair), lambda bi, hp, i, pt: (bi, 0, hp), pipeline_mode=pl.Buffered(1)),
            pl.BlockSpec((1, 1, n_heads * hd), lambda bi, hp, i, pt: (seq(bi, hp, i), 0, 0)),
            new_spec, new_spec,
            pl.BlockSpec((n_heads, page), lambda bi, hp, i, pt: (0, 0)),
            pl.BlockSpec(memory_space=pl.ANY),
            pl.BlockSpec(memory_space=pl.ANY),
        ],
        out_specs=(
            pl.BlockSpec((1, tq, pair), lambda bi, hp, i, pt: (bi, i, hp)),
            pl.BlockSpec((1, 1, n_heads * hd), lambda bi, hp, i, pt: (seq(bi, hp, i), 0, 0)),
        ),
        scratch_shapes=_attn_scratch(tq, tk, pair) + [
            pltpu.VMEM((n_heads, n_heads * hd), BF16),
            pltpu.VMEM((page, page), BF16),
            pltpu.VMEM((n_heads, page), F32),
            pltpu.VMEM((n_heads * hd, page), F32),
            pltpu.VMEM((2, n_pg, n_heads, hd, page), F32),
            pltpu.VMEM((2, n_pg, n_heads, hd, page), F32),
            pltpu.SemaphoreType.DMA((2, 2)),
        ],
    )
    a_p, a_s = pl.pallas_call(
        functools.partial(_both_attn_kernel, n_pages=n_pages, sample_steps=sample_steps, tk=tk,
                          span_widths=ATTN_SPAN_WIDTHS),
        grid_spec=grid_spec,
        out_shape=(jax.ShapeDtypeStruct((b, t, a), F32), jax.ShapeDtypeStruct((n, 1, n_heads * hd), F32)),
        compiler_params=pltpu.CompilerParams(
            dimension_semantics=("arbitrary", "arbitrary", "arbitrary"), vmem_limit_bytes=VMEM_LIMIT_BYTES),
        name="both_attn",
    )(page_table.reshape(-1), bias, qb, kb, vb, sq, k_new, v_new, sbias, cache_kt, cache_vt)
    return a_p, a_s[:, 0, :]


def _pick_tile(n, want):
    t = min(n, want)
    while n % t:
        t //= 2
    return t


def _layer_step(xp, xs, cache_k, cache_v, state_conv, page_table,
                g_pre_mix, g_post_mix, g_pre_mlp, g_post_mlp,
                w_in, attn_bias, conv_w, g_attn, g_conv, w_out, w_up, w_down):
    b, t, d = xp.shape
    n_heads = attn_bias.shape[0]
    a = n_heads * HEAD_DIM
    c = conv_w.shape[1]
    row = lambda v: v.reshape(1, -1)
    g_pre_mix, g_post_mix, g_pre_mlp, g_post_mlp = map(row, (g_pre_mix, g_post_mix, g_pre_mlp, g_post_mlp))
    g_attn, g_conv = row(g_attn), row(g_conv)

    w_in_b = w_in.astype(BF16)
    wo_a = w_out[:a].astype(BF16)
    wo_s = w_out[a:].astype(BF16)
    w_up_b = w_up.astype(BF16)
    w_dn_b = w_down.astype(BF16)

    k_p, v_p, qb, kb, vb, sn_p, tails = _prompt_proj(
        xp, g_pre_mix, w_in_b, conv_w, g_conv, a=a, tm=_pick_tile(t, 512))
    n = xs.shape[0]
    xs2 = xs.reshape(n, d)
    q_s, k_s, v_s, cu_s, sn_s = _sample_proj(
        xs2, state_conv[:, 0, :], state_conv[:, 1, :], g_pre_mix, w_in_b, conv_w, g_conv, a=a)

    tk = _pick_tile(t, ATTN_KEY_BLOCK)
    tq = _pick_tile(t, ATTN_QUERY_BLOCK)
    n_pages = page_table.shape[1]
    n_pg = _pick_tile(n_pages, SAMPLE_PAGES_PER_STEP)
    to_keys_minor = lambda cache: jnp.transpose(cache, (0, 2, 3, 1))
    sample_args = (q_s[:, None, :], k_s[:, None, :], v_s[:, None, :])
    caches = (to_keys_minor(cache_k), to_keys_minor(cache_v))
    if n * (n_pages // n_pg) == b * (a // LANES) * (t // tq):
        a_p, a_s = _both_attn(attn_bias, qb, kb, vb, page_table, *sample_args, *caches, tq=tq, tk=tk, n_pg=n_pg)
    else:
        a_p = _prompt_attn(attn_bias, qb, kb, vb, tq=tq, tk=tk)
        a_s = _paged_attn(page_table, *sample_args, attn_bias * LOG2E, *caches, n_pg=n_pg)

    y_p = _mix_mlp(xp.reshape(b * t, d), a_p.reshape(b * t, a), sn_p.reshape(b * t, c),
                   g_attn, g_post_mix, g_pre_mlp, g_post_mlp, wo_a, wo_s, w_up_b, w_dn_b,
                   tm=_pick_tile(b * t, 512))
    y_s = _mix_mlp(xs2, a_s, sn_s, g_attn, g_post_mix, g_pre_mlp, g_post_mlp,
                   wo_a, wo_s, w_up_b, w_dn_b, tm=_pick_tile(n, 512))
    new_conv_p = tails[:, -1, SUBLANES - (CONV_TAPS - 1):, :]
    new_conv_s = jnp.stack([state_conv[:, 1, :], cu_s], axis=1)

    return (y_p.reshape(b, t, d), y_s.reshape(n, 1, d),
            k_p.reshape(b, t, n_heads, HEAD_DIM), v_p.reshape(b, t, n_heads, HEAD_DIM), new_conv_p,
            k_s.reshape(n, 1, n_heads, HEAD_DIM), v_s.reshape(n, 1, n_heads, HEAD_DIM), new_conv_s)


def kernel(x_prompt, x_sample, cache_k, cache_v, state_conv, page_table, norm_pre_mix, norm_post_mix,
           norm_pre_mlp, norm_post_mlp, w_in, attn_logit_bias, conv_w, norm_attn_out, norm_conv_out,
           w_out, w_up, w_down):
    depth = w_in.shape[0]
    assert x_sample.shape[1] == 1, "the sample path handles one new token per sequence"
    xp, xs = x_prompt, x_sample
    outs = []
    for l in range(depth):
        res = _layer_step(
            xp, xs, cache_k[l], cache_v[l], state_conv[l], page_table,
            norm_pre_mix[l], norm_post_mix[l], norm_pre_mlp[l], norm_post_mlp[l],
            w_in[l], attn_logit_bias[l], conv_w[l], norm_attn_out[l], norm_conv_out[l],
            w_out[l], w_up[l], w_down[l])
        xp, xs = res[0], res[1]
        outs.append(res[2:])
    stacked = tuple(jnp.stack([o[j] for o in outs]) for j in range(6))
    return (xp, xs) + stacked
```

```python
import functools

import jax
import jax.numpy as jnp
from jax import lax
from jax.experimental import pallas as pl
from jax.experimental.pallas import tpu as pltpu

HEAD_DIM = 64
CONV_TAPS = 3
EPS = 1e-6
SUBLANES = 8
LANES = 128
VMEM_LIMIT_BYTES = 56 * 1024 * 1024
LOG2E = 1.4426950408889634
EXP2_SAFE = 126.0
ATTN_QUERY_BLOCK = 512
ATTN_KEY_BLOCK = 256
ATTN_SPAN_WIDTHS = (16, 8, 4, 2, 1)
SAMPLE_PAGES_PER_STEP = 32

F32 = jnp.float32
BF16 = jnp.bfloat16


def _rms(x, g):
    return x * lax.rsqrt(jnp.mean(x * x, axis=-1, keepdims=True) + EPS) * g


def _floor_div(x, d):
    if d & (d - 1) == 0:
        return lax.shift_right_logical(x, jnp.int32(d.bit_length() - 1))
    return x // d


def _dot(a, b):
    return jnp.dot(a, b, preferred_element_type=F32)


def _dot_nt(a, b):
    return lax.dot_general(a, b, (((1,), (1,)), ((), ())), preferred_element_type=F32)


def _neg_log2_one_minus_beta(z):
    return jnp.maximum(z, jnp.log2(1.0 + jnp.exp2(jnp.minimum(z, EXP2_SAFE))))


def _proj_kernel(x_ref, xprev_ref, g_ref, w_ref, cw_ref, gconv_ref,
                 k_ref, v_ref, qb_ref, kb_ref, vb_ref, sn_ref, tail_ref, cu_scr, *, q_scale):
    i = pl.program_id(1)
    tm = x_ref.shape[1]
    a = k_ref.shape[2]
    c = sn_ref.shape[2]
    g = g_ref[...]
    h = _rms(x_ref[0], g).astype(BF16)

    qb_ref[0] = (_dot(h, w_ref[:, 0:a]) * q_scale).astype(BF16)
    k = _dot(h, w_ref[:, a:2 * a])
    k_ref[0] = k
    kb_ref[0] = k.astype(BF16)
    v = _dot(h, w_ref[:, 2 * a:3 * a])
    v_ref[0] = v
    vb_ref[0] = v.astype(BF16)

    o = 3 * a
    gb = _dot(h, w_ref[:, o:o + c])
    cu = _dot(h, w_ref[:, o + c:o + 2 * c]) * _dot(h, w_ref[:, o + 2 * c:o + 3 * c])
    hp = _rms(xprev_ref[0], g).astype(BF16)
    cup = _dot(hp, w_ref[:, o + c:o + 2 * c]) * _dot(hp, w_ref[:, o + 2 * c:o + 3 * c])
    cu_scr[0:SUBLANES, :] = jnp.where(i == 0, 0.0, cup)
    cu_scr[SUBLANES:SUBLANES + tm, :] = cu
    y = (cw_ref[0:1, :] * cu_scr[SUBLANES - 2:SUBLANES - 2 + tm, :]
         + cw_ref[1:2, :] * cu_scr[SUBLANES - 1:SUBLANES - 1 + tm, :]
         + cw_ref[2:3, :] * cu)
    sn_ref[0] = _rms(gb * y, gconv_ref[...]).astype(BF16)
    tail_ref[0, 0] = cu[tm - SUBLANES:tm, :]


def _prompt_proj(x, g_pre, w_in, conv_w, g_conv, *, a, tm):
    b, t, d = x.shape
    c = conv_w.shape[1]
    nt = t // tm
    rows8 = tm // SUBLANES
    const2 = lambda bi, i: (0, 0)
    tile = lambda width: pl.BlockSpec((1, tm, width), lambda bi, i: (bi, i, 0))
    out_shape = (
        jax.ShapeDtypeStruct((b, t, a), F32),
        jax.ShapeDtypeStruct((b, t, a), F32),
        jax.ShapeDtypeStruct((b, t, a), BF16),
        jax.ShapeDtypeStruct((b, t, a), BF16),
        jax.ShapeDtypeStruct((b, t, a), BF16),
        jax.ShapeDtypeStruct((b, t, c), BF16),
        jax.ShapeDtypeStruct((b, nt, SUBLANES, c), F32),
    )
    return pl.pallas_call(
        functools.partial(_proj_kernel, q_scale=HEAD_DIM ** -0.5 * LOG2E),
        grid=(b, nt),
        in_specs=[
            tile(d),
            pl.BlockSpec((1, SUBLANES, d), lambda bi, i: (bi, jnp.maximum(i * rows8 - 1, 0), 0)),
            pl.BlockSpec(g_pre.shape, const2),
            pl.BlockSpec(w_in.shape, const2),
            pl.BlockSpec(conv_w.shape, const2),
            pl.BlockSpec(g_conv.shape, const2),
        ],
        out_specs=(tile(a), tile(a), tile(a), tile(a), tile(a), tile(c),
                   pl.BlockSpec((1, 1, SUBLANES, c), lambda bi, i: (bi, i, 0, 0))),
        out_shape=out_shape,
        scratch_shapes=[pltpu.VMEM((tm + SUBLANES, c), F32)],
        compiler_params=pltpu.CompilerParams(
            dimension_semantics=("parallel", "arbitrary"), vmem_limit_bytes=VMEM_LIMIT_BYTES),
        name="prompt_proj",
    )(x, x, g_pre, w_in, conv_w, g_conv)


def _attn_kernel(bias_ref, q_ref, k_ref, v_ref, o_ref, q_scr, tri_scr, carry_scr, acc_scr, *, tk, span_widths):
    _attn_step(pl.program_id(1), pl.program_id(2), bias_ref, q_ref, k_ref, v_ref, o_ref,
               q_scr, tri_scr, carry_scr, acc_scr, tk=tk, span_widths=span_widths)


def _attn_scratch(tq, tk, pair):
    n_heads_blk = pair // HEAD_DIM
    return [
        pltpu.VMEM((n_heads_blk, tq, pair), BF16),
        pltpu.VMEM((tk, tk), BF16),
        pltpu.VMEM((n_heads_blk, tq, LANES), F32),
        pltpu.VMEM((n_heads_blk, tq, pair), F32),
    ]


def _attn_step(hp, i, bias_ref, q_ref, k_ref, v_ref, o_ref, q_scr, tri_scr, carry_scr, acc_scr,
               *, tk, span_widths, diag_hooks=()):
    tq, pair = q_ref.shape[1], q_ref.shape[2]
    n_heads_blk = pair // HEAD_DIM

    @pl.when(i == 0)
    def _():
        r = lax.broadcasted_iota(jnp.int32, (tk, tk), 0)
        cidx = lax.broadcasted_iota(jnp.int32, (tk, tk), 1)
        tri_scr[...] = jnp.where(r >= cidx, 1.0, 0.0).astype(BF16)

    q = q_ref[0]
    lane = lax.broadcasted_iota(jnp.int32, (tq, pair), 1)
    for hh in range(n_heads_blk):
        in_head = (lane >= hh * HEAD_DIM) & (lane < (hh + 1) * HEAD_DIM)
        q_scr[hh] = jnp.where(in_head, q, jnp.zeros_like(q))
    acc_scr[...] = jnp.zeros_like(acc_scr)
    carry_scr[...] = jnp.zeros_like(carry_scr)
    biases = [bias_ref[n_heads_blk * hp + hh] * LOG2E for hh in range(n_heads_blk)]

    def span(j_lo, n_sub, mask, hooks=()):
        ks = pl.ds(pl.multiple_of(j_lo * tk, tk), n_sub * tk)
        kb = k_ref[0, ks, :]
        vb = v_ref[0, ks, :]
        heads = range(n_heads_blk)
        carries = [carry_scr[hh] for hh in heads]
        accs = [acc_scr[hh] for hh in heads]
        zs, pbs = {}, {}

        def logits(s):
            rows = slice(s * tk, (s + 1) * tk)
            for hh in heads:
                zs[s, hh] = _dot_nt(q_scr[hh], kb[rows, :]) + biases[hh]

        def neg_log(s):
            for hh in heads:
                p = _neg_log2_one_minus_beta(zs[s, hh])
                if mask is not None:
                    p = jnp.where(mask(s), p, 0.0)
                pbs[s, hh] = p.astype(BF16)

        def weigh(s):
            rows = slice(s * tk, (s + 1) * tk)
            for hh in heads:
                incl = _dot(pbs.pop((s, hh)), tri_scr[...])
                carry = jnp.concatenate([carries[hh]] * (tk // LANES), axis=1)
                w = jnp.exp2(zs.pop((s, hh)) - incl - carry)
                if mask is not None:
                    w = jnp.where(mask(s), w, 0.0)
                accs[hh] = accs[hh] + _dot(w.astype(BF16), vb[rows, :])
                carries[hh] = carries[hh] + incl[:, 0:1]

        order = list(reversed(range(n_sub)))
        if hooks:
            hooks[0]()
        for step in range(n_sub + 2):
            if step < n_sub:
                logits(order[step])
            if 1 <= step <= n_sub:
                neg_log(order[step - 1])
            if step >= 2:
                weigh(order[step - 2])
            if step + 1 < len(hooks):
                hooks[step + 1]()
        for hh in heads:
            acc_scr[hh] = accs[hh]
            carry_scr[hh] = carries[hh]

    n_diag = tq // tk
    qr = lax.broadcasted_iota(jnp.int32, (tq, tk), 0)
    kc = lax.broadcasted_iota(jnp.int32, (tq, tk), 1)
    span(i * n_diag, n_diag, lambda s: kc + s * tk < qr, diag_hooks)

    left = i * n_diag
    for width in span_widths:
        n_steps = left // width

        def body(s, _, hi=left, width=width):
            span(hi - (s + 1) * width, width, None)
            return 0
        lax.fori_loop(0, n_steps, body, 0)
        left = left - n_steps * width

    out = acc_scr[0]
    for hh in range(1, n_heads_blk):
        out = jnp.where(lane >= hh * HEAD_DIM, acc_scr[hh], out)
    o_ref[0] = out


def _prompt_attn(bias, qb, kb, vb, *, tq, tk):
    b, t, a = qb.shape
    assert tq % tk == 0
    pair = LANES
    return pl.pallas_call(
        functools.partial(_attn_kernel, tk=tk, span_widths=ATTN_SPAN_WIDTHS),
        grid=(b, a // pair, t // tq),
        in_specs=[
            pl.BlockSpec(memory_space=pltpu.SMEM),
            pl.BlockSpec((1, tq, pair), lambda bi, hp, i: (bi, i, hp)),
            pl.BlockSpec((1, t, pair), lambda bi, hp, i: (bi, 0, hp)),
            pl.BlockSpec((1, t, pair), lambda bi, hp, i: (bi, 0, hp)),
        ],
        out_specs=pl.BlockSpec((1, tq, pair), lambda bi, hp, i: (bi, i, hp)),
        out_shape=jax.ShapeDtypeStruct((b, t, a), F32),
        scratch_shapes=_attn_scratch(tq, tk, pair),
        compiler_params=pltpu.CompilerParams(
            dimension_semantics=("parallel", "parallel", "arbitrary"),
            vmem_limit_bytes=VMEM_LIMIT_BYTES),
        name="prompt_attn",
    )(bias, qb, kb, vb)


def _mix_mlp_kernel(x_ref, a_ref, sn_ref, ga_ref, gpm_ref, gpre_ref, gpost_ref,
                    woa_ref, wos_ref, wup_ref, wdn_ref, y_ref, *, ff_chunk):
    an = _rms(a_ref[...], ga_ref[...]).astype(BF16)
    mix = _dot(an, woa_ref[...]) + _dot(sn_ref[...], wos_ref[...])
    x1 = x_ref[...] + _rms(mix, gpm_ref[...])
    hm = _rms(x1, gpre_ref[...]).astype(BF16)
    m = jnp.zeros_like(x1)
    for j in range(wup_ref.shape[1] // ff_chunk):
        up = jnp.maximum(_dot(hm, wup_ref[:, j * ff_chunk:(j + 1) * ff_chunk]), 0.0)
        m = m + _dot((up * up).astype(BF16), wdn_ref[j * ff_chunk:(j + 1) * ff_chunk, :])
    y_ref[...] = x1 + _rms(m, gpost_ref[...])


def _mix_mlp(x, a, sn, g_attn, g_post_mix, g_pre_mlp, g_post_mlp, wo_a, wo_s, w_up, w_dn, *, tm):
    n, d = x.shape
    const = lambda i: (0, 0)
    resident = lambda arr: pl.BlockSpec(arr.shape, const, pipeline_mode=pl.Buffered(1))
    return pl.pallas_call(
        functools.partial(_mix_mlp_kernel, ff_chunk=min(1024, w_up.shape[1])),
        grid=(n // tm,),
        in_specs=[
            pl.BlockSpec((tm, d), lambda i: (i, 0)),
            pl.BlockSpec((tm, a.shape[1]), lambda i: (i, 0)),
            pl.BlockSpec((tm, sn.shape[1]), lambda i: (i, 0)),
            resident(g_attn), resident(g_post_mix), resident(g_pre_mlp), resident(g_post_mlp),
            resident(wo_a), resident(wo_s), resident(w_up), resident(w_dn),
        ],
        out_specs=pl.BlockSpec((tm, d), lambda i: (i, 0)),
        out_shape=jax.ShapeDtypeStruct((n, d), F32),
        compiler_params=pltpu.CompilerParams(
            dimension_semantics=("parallel",), vmem_limit_bytes=VMEM_LIMIT_BYTES),
        name="mix_mlp",
    )(x, a, sn, g_attn, g_post_mix, g_pre_mlp, g_post_mlp, wo_a, wo_s, w_up, w_dn)


def _sample_proj_kernel(x_ref, st0_ref, st1_ref, g_ref, w_ref, cw_ref, gconv_ref,
                        q_ref, k_ref, v_ref, cu_ref, sn_ref, *, scale):
    a = k_ref.shape[1]
    c = sn_ref.shape[1]
    h = _rms(x_ref[...], g_ref[...]).astype(BF16)
    q_ref[...] = _dot(h, w_ref[:, 0:a]) * scale
    k_ref[...] = _dot(h, w_ref[:, a:2 * a])
    v_ref[...] = _dot(h, w_ref[:, 2 * a:3 * a])
    gb = _dot(h, w_ref[:, 3 * a:3 * a + c])
    cu = _dot(h, w_ref[:, 3 * a + c:3 * a + 2 * c]) * _dot(h, w_ref[:, 3 * a + 2 * c:3 * a + 3 * c])
    cu_ref[...] = cu
    y = cw_ref[0:1, :] * st0_ref[...] + cw_ref[1:2, :] * st1_ref[...] + cw_ref[2:3, :] * cu
    sn_ref[...] = _rms(gb * y, gconv_ref[...]).astype(BF16)


def _sample_proj(x, st0, st1, g_pre, w_in, conv_w, g_conv, *, a):
    n = x.shape[0]
    c = conv_w.shape[1]
    return pl.pallas_call(
        functools.partial(_sample_proj_kernel, scale=HEAD_DIM ** -0.5 * LOG2E),
        out_shape=(
            jax.ShapeDtypeStruct((n, a), F32), jax.ShapeDtypeStruct((n, a), F32),
            jax.ShapeDtypeStruct((n, a), F32), jax.ShapeDtypeStruct((n, c), F32),
            jax.ShapeDtypeStruct((n, c), BF16),
        ),
        compiler_params=pltpu.CompilerParams(vmem_limit_bytes=VMEM_LIMIT_BYTES),
        name="sample_proj",
    )(x, st0, st1, g_pre, w_in, conv_w, g_conv)


def _paged_attn_kernel(pt_ref, q_ref, knew_ref, vnew_ref, bias_ref, *rest, n_pg):
    del pt_ref
    k_refs = rest[:n_pg]
    v_refs = rest[n_pg:2 * n_pg]
    o_ref = rest[2 * n_pg]
    scratch = rest[2 * n_pg + 1:]
    first, logits, weigh, values, last = _paged_attn_stages(
        pl.program_id(1), pl.num_programs(1), q_ref, knew_ref, vnew_ref, bias_ref,
        [r.at[0] for r in k_refs], [r.at[0] for r in v_refs], o_ref, *scratch)
    first()
    logits()
    weigh()
    values()
    last()


def _paged_attn_stages(cstep, n_steps, q_ref, knew_ref, vnew_ref, bias_ref, k_pages, v_pages, o_ref,
                       qbd_scr, tri_scr, carry_scr, acc_scr):
    n_pg = len(k_pages)
    n_heads, hd, page = k_pages[0].shape
    a = n_heads * hd
    bias = bias_ref[...]

    def weights(z, carry, mask):
        p = _neg_log2_one_minus_beta(z)
        if mask is not None:
            p = jnp.where(mask, p, 0.0)
        incl = _dot(p.astype(BF16), tri_scr[...])
        w = jnp.exp2(z - incl - carry)
        if mask is not None:
            w = jnp.where(mask, w, 0.0)
        return w, incl[:, 0:1]

    def first():
        pl.when(cstep == 0)(start_sequence)

    def start_sequence():
        sub = lax.broadcasted_iota(jnp.int32, (n_heads, a), 0)
        lane = lax.broadcasted_iota(jnp.int32, (n_heads, a), 1)
        q_all = jnp.broadcast_to(q_ref[0], (n_heads, a))
        qbd_scr[...] = jnp.where(lane // hd == sub, q_all, 0.0).astype(BF16)
        r = lax.broadcasted_iota(jnp.int32, (page, page), 0)
        c = lax.broadcasted_iota(jnp.int32, (page, page), 1)
        tri_scr[...] = jnp.where(r >= c, 1.0, 0.0).astype(BF16)
        n_new = knew_ref.shape[1]
        assert n_new == 1, "one new key per sequence"
        slot = lax.broadcasted_iota(jnp.int32, (n_heads, page), 1)
        q_off = 0
        visible = (slot < q_off) & (slot < n_new)
        k_new = jnp.broadcast_to(knew_ref[0], (page, a)).T.astype(BF16)
        v_new = jnp.broadcast_to(vnew_ref[0], (page, a)).T
        z = _dot(qbd_scr[...], k_new) + bias
        w, tot = weights(z, jnp.zeros((n_heads, page), F32), visible)
        carry_scr[...] = jnp.broadcast_to(tot, (n_heads, page))
        w_rows = jnp.concatenate([jnp.broadcast_to(w[h:h + 1, :], (hd, page)) for h in range(n_heads)], axis=0)
        acc_scr[...] = w_rows * v_new

    state = {}

    def logits():
        qbd = qbd_scr[...]
        state["zs"] = [_dot(qbd, k_pages[r][...].reshape(a, page).astype(BF16)) + bias for r in range(n_pg)]

    def weigh():
        z = jnp.concatenate(state["zs"], axis=0)
        p = _neg_log2_one_minus_beta(z)
        incl = _dot(p.astype(BF16), tri_scr[...])
        carry = carry_scr[...]
        ws = [None] * n_pg
        for r in reversed(range(n_pg)):
            rows = slice(r * n_heads, (r + 1) * n_heads)
            ws[r] = jnp.exp2(state["zs"][r] - incl[rows] - carry)
            carry = carry + incl[rows, 0:1]
        carry_scr[...] = carry
        state["ws"] = ws

    def values():
        for h in range(n_heads):
            rows = slice(h * hd, (h + 1) * hd)
            acc = acc_scr[rows, :]
            for r in range(n_pg):
                acc = acc + jnp.broadcast_to(state["ws"][r][h:h + 1, :], (hd, page)) * v_pages[r][h]
            acc_scr[rows, :] = acc

    def last():
        pl.when(cstep == n_steps - 1)(end_sequence)

    def end_sequence():
        col = jnp.sum(acc_scr[...], axis=1, keepdims=True)
        o_ref[0] = jnp.broadcast_to(col, (a, page)).T[0:1, :]

    return first, logits, weigh, values, last


def _paged_attn(page_table, q, k_new, v_new, bias, cache_kt, cache_vt, *, n_pg):
    n, _, a = q.shape
    n_pages = page_table.shape[1]
    n_heads, hd, page = cache_kt.shape[1:]
    assert hd == HEAD_DIM and a == n_heads * hd
    n_steps = n_pages // n_pg
    bias_tile = jnp.broadcast_to(bias[:, None], (n_heads, page))

    def page_map(bi, c, pt, *, r):
        return (pt[bi * n_pages + (n_steps - 1 - c) * n_pg + r], 0, 0, 0)

    page_specs = [pl.BlockSpec((1, n_heads, hd, page), functools.partial(page_map, r=r))
                  for r in range(n_pg)]
    new_spec = pl.BlockSpec((1,) + k_new.shape[1:], lambda bi, c, pt: (bi, 0, 0))
    grid_spec = pltpu.PrefetchScalarGridSpec(
        num_scalar_prefetch=1,
        grid=(n, n_steps),
        in_specs=[pl.BlockSpec((1, 1, a), lambda bi, c, pt: (bi, 0, 0)), new_spec, new_spec,
                  pl.BlockSpec((n_heads, page), lambda bi, c, pt: (0, 0))] + page_specs + page_specs,
        out_specs=pl.BlockSpec((1, 1, a), lambda bi, c, pt: (bi, 0, 0)),
        scratch_shapes=[
            pltpu.VMEM((n_heads, a), BF16),
            pltpu.VMEM((page, page), BF16),
            pltpu.VMEM((n_heads, page), F32),
            pltpu.VMEM((a, page), F32),
        ],
    )
    out = pl.pallas_call(
        functools.partial(_paged_attn_kernel, n_pg=n_pg),
        grid_spec=grid_spec,
        out_shape=jax.ShapeDtypeStruct((n, 1, a), F32),
        compiler_params=pltpu.CompilerParams(
            dimension_semantics=("parallel", "arbitrary"), vmem_limit_bytes=VMEM_LIMIT_BYTES),
        name="sample_attn",
    )(page_table.reshape(-1), q, k_new, v_new, bias_tile,
      *([cache_kt] * n_pg), *([cache_vt] * n_pg))
    return out[:, 0, :]


def _both_attn_kernel(pt_ref, bias_ref, q_ref, k_ref, v_ref, sq_ref, knew_ref, vnew_ref, sbias_ref,
                      cache_k_ref, cache_v_ref, o_ref, so_ref,
                      q_scr, tri_scr, carry_scr, acc_scr, sqbd_scr, stri_scr, scarry_scr, sacc_scr,
                      kbuf, vbuf, sem, *, n_pages, sample_steps, tk, span_widths):
    n_pg = kbuf.shape[1]
    hp, i = pl.program_id(1), pl.program_id(2)
    n_grid = pl.num_programs(0) * pl.num_programs(1) * pl.num_programs(2)
    step = (pl.program_id(0) * pl.num_programs(1) + hp) * pl.num_programs(2) + i
    slot = step & 1

    def page_copies(to_slot, s=None):
        if s is not None:
            seq_idx = _floor_div(s, sample_steps)
            first_page = seq_idx * n_pages + (sample_steps - 1 - (s - seq_idx * sample_steps)) * n_pg
        copies = []
        for r in range(n_pg):
            phys = 0 if s is None else pt_ref[first_page + r]
            copies.append(pltpu.make_async_copy(cache_k_ref.at[phys], kbuf.at[to_slot, r], sem.at[to_slot, 0]))
            copies.append(pltpu.make_async_copy(cache_v_ref.at[phys], vbuf.at[to_slot, r], sem.at[to_slot, 1]))
        return copies

    @pl.when(step == 0)
    def _():
        for c in page_copies(slot, step):
            c.start()

    @pl.when(step + 1 < n_grid)
    def _():
        for c in page_copies(1 - slot, step + 1):
            c.start()

    for c in page_copies(slot):
        c.wait()

    first, logits, weigh, values, last = _paged_attn_stages(
        step - _floor_div(step, sample_steps) * sample_steps, sample_steps,
        sq_ref, knew_ref, vnew_ref, sbias_ref,
        [kbuf.at[slot, r] for r in range(n_pg)], [vbuf.at[slot, r] for r in range(n_pg)], so_ref,
        sqbd_scr, stri_scr, scarry_scr, sacc_scr)
    first()
    _attn_step(hp, i, bias_ref, q_ref, k_ref, v_ref, o_ref, q_scr, tri_scr, carry_scr, acc_scr,
               tk=tk, span_widths=span_widths, diag_hooks=(logits, weigh, values))
    last()


def _both_attn(bias, qb, kb, vb, page_table, sq, k_new, v_new, cache_kt, cache_vt, *, tq, tk, n_pg):
    b, t, a = qb.shape
    assert tq % tk == 0
    pair = LANES
    n = sq.shape[0]
    n_pages = page_table.shape[1]
    n_heads, hd, page = cache_kt.shape[1:]
    n_hp, n_q = a // pair, t // tq
    sample_steps = n_pages // n_pg
    assert n * sample_steps == b * n_hp * n_q and hd == HEAD_DIM
    sbias = jnp.broadcast_to((bias * LOG2E)[:, None], (n_heads, page))

    def seq(bi, hp, i):
        return _floor_div((bi * n_hp + hp) * n_q + i, sample_steps)

    new_spec = pl.BlockSpec((1,) + k_new.shape[1:], lambda bi, hp, i, pt: (seq(bi, hp, i), 0, 0))
    grid_spec = pltpu.PrefetchScalarGridSpec(
        num_scalar_prefetch=1,
        grid=(b, n_hp, n_q),
        in_specs=[
            pl.BlockSpec(memory_space=pltpu.SMEM),
            pl.BlockSpec((1, tq, pair), lambda bi, hp, i, pt: (bi, i, hp)),
            pl.BlockSpec((1, t, pair), lambda bi, hp, i, pt: (bi, 0, hp), pipeline_mode=pl.Buffered(1)),
            pl.BlockSpec((1, t, pair), lambda bi, hp, i, pt: (bi, 0, hp), pipeline_mode=pl.Buffered(1)),
            pl.BlockSpec((1, 1, n_heads * hd), lambda bi, hp, i, pt: (seq(bi, hp, i), 0, 0)),
            new_spec, new_spec,
            pl.BlockSpec((n_heads, page), lambda bi, hp, i, pt: (0, 0)),
            pl.BlockSpec(memory_space=pl.ANY),
            pl.BlockSpec(memory_space=pl.ANY),
        ],
        out_specs=(
            pl.BlockSpec((1, tq, pair), lambda bi, hp, i, pt: (bi, i, hp)),
            pl.BlockSpec((1, 1, n_heads * hd), lambda bi, hp, i, pt: (seq(bi, hp, i), 0, 0)),
        ),
        scratch_shapes=_attn_scratch(tq, tk, pair) + [
            pltpu.VMEM((n_heads, n_heads * hd), BF16),
            pltpu.VMEM((page, page), BF16),
            pltpu.VMEM((n_heads, page), F32),
            pltpu.VMEM((n_heads * hd, page), F32),
            pltpu.VMEM((2, n_pg, n_heads, hd, page), F32),
            pltpu.VMEM((2, n_pg, n_heads, hd, page), F32),
            pltpu.SemaphoreType.DMA((2, 2)),
        ],
    )
    a_p, a_s = pl.pallas_call(
        functools.partial(_both_attn_kernel, n_pages=n_pages, sample_steps=sample_steps, tk=tk,
                          span_widths=ATTN_SPAN_WIDTHS),
        grid_spec=grid_spec,
        out_shape=(jax.ShapeDtypeStruct((b, t, a), F32), jax.ShapeDtypeStruct((n, 1, n_heads * hd), F32)),
        compiler_params=pltpu.CompilerParams(
            dimension_semantics=("arbitrary", "arbitrary", "arbitrary"), vmem_limit_bytes=VMEM_LIMIT_BYTES),
        name="both_attn",
    )(page_table.reshape(-1), bias, qb, kb, vb, sq, k_new, v_new, sbias, cache_kt, cache_vt)
    return a_p, a_s[:, 0, :]


def _pick_tile(n, want):
    t = min(n, want)
    while n % t:
        t //= 2
    return t


def _layer_step(xp, xs, cache_k, cache_v, state_conv, page_table,
                g_pre_mix, g_post_mix, g_pre_mlp, g_post_mlp,
                w_in, attn_bias, conv_w, g_attn, g_conv, w_out, w_up, w_down):
    b, t, d = xp.shape
    n_heads = attn_bias.shape[0]
    a = n_heads * HEAD_DIM
    c = conv_w.shape[1]
    row = lambda v: v.reshape(1, -1)
    g_pre_mix, g_post_mix, g_pre_mlp, g_post_mlp = map(row, (g_pre_mix, g_post_mix, g_pre_mlp, g_post_mlp))
    g_attn, g_conv = row(g_attn), row(g_conv)

    w_in_b = w_in.astype(BF16)
    wo_a = w_out[:a].astype(BF16)
    wo_s = w_out[a:].astype(BF16)
    w_up_b = w_up.astype(BF16)
    w_dn_b = w_down.astype(BF16)

    k_p, v_p, qb, kb, vb, sn_p, tails = _prompt_proj(
        xp, g_pre_mix, w_in_b, conv_w, g_conv, a=a, tm=_pick_tile(t, 1024))
    n = xs.shape[0]
    xs2 = xs.reshape(n, d)
    q_s, k_s, v_s, cu_s, sn_s = _sample_proj(
        xs2, state_conv[:, 0, :], state_conv[:, 1, :], g_pre_mix, w_in_b, conv_w, g_conv, a=a)

    tk = _pick_tile(t, ATTN_KEY_BLOCK)
    tq = _pick_tile(t, ATTN_QUERY_BLOCK)
    n_pages = page_table.shape[1]
    n_pg = _pick_tile(n_pages, SAMPLE_PAGES_PER_STEP)
    to_keys_minor = lambda cache: jnp.transpose(cache, (0, 2, 3, 1))
    sample_args = (q_s[:, None, :], k_s[:, None, :], v_s[:, None, :])
    caches = (to_keys_minor(cache_k), to_keys_minor(cache_v))
    if n * (n_pages // n_pg) == b * (a // LANES) * (t // tq):
        a_p, a_s = _both_attn(attn_bias, qb, kb, vb, page_table, *sample_args, *caches, tq=tq, tk=tk, n_pg=n_pg)
    else:
        a_p = _prompt_attn(attn_bias, qb, kb, vb, tq=tq, tk=tk)
        a_s = _paged_attn(page_table, *sample_args, attn_bias * LOG2E, *caches, n_pg=n_pg)

    y_p = _mix_mlp(xp.reshape(b * t, d), a_p.reshape(b * t, a), sn_p.reshape(b * t, c),
                   g_attn, g_post_mix, g_pre_mlp, g_post_mlp, wo_a, wo_s, w_up_b, w_dn_b,
                   tm=_pick_tile(b * t, 512))
    y_s = _mix_mlp(xs2, a_s, sn_s, g_attn, g_post_mix, g_pre_mlp, g_post_mlp,
                   wo_a, wo_s, w_up_b, w_dn_b, tm=_pick_tile(n, 512))
    new_conv_p = tails[:, -1, SUBLANES - (CONV_TAPS - 1):, :]
    new_conv_s = jnp.stack([state_conv[:, 1, :], cu_s], axis=1)

    return (y_p.reshape(b, t, d), y_s.reshape(n, 1, d),
            k_p.reshape(b, t, n_heads, HEAD_DIM), v_p.reshape(b, t, n_heads, HEAD_DIM), new_conv_p,
            k_s.reshape(n, 1, n_heads, HEAD_DIM), v_s.reshape(n, 1, n_heads, HEAD_DIM), new_conv_s)


def kernel(x_prompt, x_sample, cache_k, cache_v, state_conv, page_table, norm_pre_mix, norm_post_mix,
           norm_pre_mlp, norm_post_mlp, w_in, attn_logit_bias, conv_w, norm_attn_out, norm_conv_out,
           w_out, w_up, w_down):
    depth = w_in.shape[0]
    assert x_sample.shape[1] == 1, "the sample path handles one new token per sequence"
    xp, xs = x_prompt, x_sample
    outs = []
    for l in range(depth):
        res = _layer_step(
            xp, xs, cache_k[l], cache_v[l], state_conv[l], page_table,
            norm_pre_mix[l], norm_post_mix[l], norm_pre_mlp[l], norm_post_mlp[l],
            w_in[l], attn_logit_bias[l], conv_w[l], norm_attn_out[l], norm_conv_out[l],
            w_out[l], w_up[l], w_down[l])
        xp, xs = res[0], res[1]
        outs.append(res[2:])
    stacked = tuple(jnp.stack([o[j] for o in outs]) for j in range(6))
    return (xp, xs) + stacked
```
